```python
import jax, jax.numpy as jnp
from jax import lax
import numpy as np

D_MODEL = 1024
BATCH = 8
SEQ = 2048
DEPTH = 1
DEC_BATCH = 16
DEC_SEQ = 4096
PAST_LEN = 128

MIX_WIDTH = D_MODEL
ATTN_WIDTH = MIX_WIDTH // 2
HEAD_DIM = 64
N_HEADS = ATTN_WIDTH // HEAD_DIM
N_KV_HEADS = 2
KV_GROUP = N_HEADS // N_KV_HEADS
KV_WIDTH = N_KV_HEADS * HEAD_DIM
WINDOW = 128
ATTN_BLOCK = 128
HG_WIDTH = MIX_WIDTH - ATTN_WIDTH
HG_EXPAND = 128
HG_HEADS = HG_WIDTH // HG_EXPAND
HG_DK = HG_EXPAND
HG_DV = HG_WIDTH // HG_HEADS
HG_KDIM = HG_HEADS * HG_DK
HG_CHUNK = 64
D_FF = 2816
ALPHA = (2.0 * DEPTH) ** 0.25
BETA_INIT = (8.0 * DEPTH) ** -0.25
EPS = 1e-5
IN_SIZES = (ATTN_WIDTH, KV_WIDTH, KV_WIDTH, HG_KDIM, HG_KDIM, HG_KDIM, HG_WIDTH, HG_WIDTH)
IN_COLS = ATTN_WIDTH + 2 * KV_WIDTH + 3 * HG_KDIM + 2 * HG_WIDTH

kernel_name = "hymba_swa_hgrn2_macaron_deepnorm_encoder"


def layer_norm(x, g, b):
    xf = x.astype(jnp.float32)
    mu = jnp.mean(xf, axis=-1, keepdims=True)
    var = jnp.mean(jnp.square(xf - mu), axis=-1, keepdims=True)
    y = (xf - mu) * lax.rsqrt(var + EPS) * g.astype(jnp.float32) + b.astype(jnp.float32)
    return y.astype(x.dtype)


def group_rms_norm(x, g, n_groups):
    shp = x.shape
    xf = x.astype(jnp.float32).reshape(shp[:-1] + (n_groups, shp[-1] // n_groups))
    xf = xf * lax.rsqrt(jnp.mean(jnp.square(xf), axis=-1, keepdims=True) + EPS)
    return (xf.reshape(shp) * g.astype(jnp.float32)).astype(x.dtype)


def swiglu_ffn(x, w13, w2):
    gate, up = jnp.split(x @ w13, 2, axis=-1)
    return (jax.nn.silu(gate) * up) @ w2


def split_in_proj(proj):
    outs, off = [], 0
    for s in IN_SIZES:
        outs.append(proj[..., off:off + s])
        off += s
    return outs


def windowed_gqa_attention(q, k, v, sink):
    B, T = q.shape[0], q.shape[1]
    nb = T // ATTN_BLOCK
    span = ATTN_BLOCK + 2 * WINDOW
    qb = jnp.moveaxis(q.reshape(B, nb, ATTN_BLOCK, N_KV_HEADS, KV_GROUP, HEAD_DIM), 1, 0)
    pad = ((0, 0), (WINDOW, WINDOW), (0, 0), (0, 0))
    kp = jnp.pad(k, pad)
    vp = jnp.pad(v, pad)
    slopes = 2.0 ** (-8.0 * jnp.arange(1, N_HEADS + 1, dtype=jnp.float32) / N_HEADS)
    slopes = slopes.reshape(1, N_KV_HEADS, KV_GROUP, 1, 1)
    sink5 = sink.astype(jnp.float32).reshape(1, N_KV_HEADS, KV_GROUP, 1, 1)
    scale = HEAD_DIM ** -0.5
    offs_q = jnp.arange(ATTN_BLOCK)
    offs_k = jnp.arange(span) - WINDOW
    rel = offs_k[None, :] - offs_q[:, None]
    alibi = slopes * jnp.abs(rel).astype(jnp.float32)

    def block(args):
        qi, i = args
        start = i * ATTN_BLOCK
        ki = lax.dynamic_slice_in_dim(kp, start, span, axis=1)
        vi = lax.dynamic_slice_in_dim(vp, start, span, axis=1)
        s_pos = start + offs_k
        valid = (jnp.abs(rel) <= WINDOW) & ((s_pos >= 0) & (s_pos < T))[None, :]
        logits = jnp.einsum('bqkgd,bskd->bkgqs', qi, ki,
                            preferred_element_type=jnp.float32) * scale - alibi
        logits = jnp.where(valid, logits, -jnp.inf)
        sink_col = jnp.broadcast_to(sink5, logits.shape[:-1] + (1,))
        p = jax.nn.softmax(jnp.concatenate([logits, sink_col], axis=-1), axis=-1)[..., :-1]
        return jnp.einsum('bkgqs,bskd->bqkgd', p.astype(vi.dtype), vi)

    out = lax.map(block, (qb, jnp.arange(nb)))
    return jnp.moveaxis(out, 0, 1).reshape(B, T, N_HEADS * HEAD_DIM)


def gla_chunk_scan(q, k, v, g):
    B, T, H, dk = q.shape
    dv = v.shape[-1]
    n = T // HG_CHUNK

    def to_chunks(a):
        return jnp.moveaxis(a.reshape(B, n, HG_CHUNK, H, a.shape[-1]), 1, 0)

    causal = jnp.tril(jnp.ones((HG_CHUNK, HG_CHUNK), dtype=bool))[None, :, :, None, None]

    def step(S, xs):
        qc, kc, vc, gc = xs
        b = jnp.cumsum(gc, axis=1)
        diff = b[:, :, None] - b[:, None, :]
        decay = jnp.exp(jnp.where(causal, diff, -jnp.inf))
        A = jnp.einsum('bthd,bshd,btshd->bhts', qc, kc, decay)
        o_intra = jnp.einsum('bhts,bshv->bthv', A, vc)
        o_inter = jnp.einsum('bthd,bhdv->bthv', qc * jnp.exp(b), S)
        b_last = b[:, -1]
        k_dec = kc * jnp.exp(b_last[:, None] - b)
        S_new = jnp.exp(b_last)[..., None] * S + jnp.einsum('bshd,bshv->bhdv', k_dec, vc)
        return S_new, o_intra + o_inter

    S0 = jnp.zeros((B, H, dk, dv), jnp.float32)
    _, o = lax.scan(step, S0, (to_chunks(q), to_chunks(k), to_chunks(v), to_chunks(g)))
    return jnp.moveaxis(o, 0, 1).reshape(B, T, H, dv)


def hgrn2_gate(f_logit, lb_param, layer):
    lb = jnp.cumsum(jax.nn.softmax(lb_param.astype(jnp.float32), axis=0), axis=0)[layer]
    s = jax.nn.sigmoid(f_logit.astype(jnp.float32))
    log_f = jnp.log(lb + (1.0 - lb) * s)
    k = (1.0 - lb) * (1.0 - s)
    return log_f, k


def hgrn2_bidirectional(q, f_fwd, f_bwd, i_in, lb_fwd, lb_bwd, layer):
    B, T = q.shape[0], q.shape[1]
    heads = lambda a, d: a.astype(jnp.float32).reshape(B, T, HG_HEADS, d)
    qh = heads(q, HG_DK)
    vh = heads(i_in, HG_DV)
    gf, kf = hgrn2_gate(f_fwd, lb_fwd, layer)
    gb, kb = hgrn2_gate(f_bwd, lb_bwd, layer)
    gf, kf, gb, kb = heads(gf, HG_DK), heads(kf, HG_DK), heads(gb, HG_DK), heads(kb, HG_DK)
    o_f = gla_chunk_scan(qh, kf, vh, gf)
    flip = lambda a: jnp.flip(a, axis=1)
    o_b = flip(gla_chunk_scan(flip(qh), flip(kb), flip(vh), flip(gb)))
    return (o_f + o_b).reshape(B, T, HG_WIDTH)


def token_mixer(h, w_in, w_out, sink, attn_norm_g, lb_fwd, lb_bwd, hg_norm_g, layer):
    B, T, _ = h.shape
    q_a, k_a, v_a, q_h, f_f, f_b, i_h, g_h = split_in_proj(h @ w_in)
    o_attn = windowed_gqa_attention(q_a.reshape(B, T, N_HEADS, HEAD_DIM),
                                    k_a.reshape(B, T, N_KV_HEADS, HEAD_DIM),
                                    v_a.reshape(B, T, N_KV_HEADS, HEAD_DIM), sink)
    o_attn = group_rms_norm(o_attn, attn_norm_g, 1)
    o_hg = hgrn2_bidirectional(q_h, f_f, f_b, i_h, lb_fwd, lb_bwd, layer).astype(h.dtype)
    o_hg = group_rms_norm(o_hg, hg_norm_g, HG_HEADS) * jax.nn.silu(g_h)
    return jnp.concatenate([o_attn, o_hg], axis=-1) @ w_out


def trunk(x, ln_g, ln_b, ffn_w13, ffn_w2, w_in, attn_sink, attn_norm_g,
          hg_lb_fwd, hg_lb_bwd, hg_norm_g, w_out):
    for l in range(DEPTH):
        x = layer_norm(ALPHA * x + 0.5 * swiglu_ffn(x, ffn_w13[l, 0], ffn_w2[l, 0]),
                       ln_g[l, 0], ln_b[l, 0])
        mix = token_mixer(x, w_in[l], w_out[l], attn_sink[l], attn_norm_g[l],
                          hg_lb_fwd, hg_lb_bwd, hg_norm_g[l], l)
        x = layer_norm(ALPHA * x + mix, ln_g[l, 1], ln_b[l, 1])
        x = layer_norm(ALPHA * x + 0.5 * swiglu_ffn(x, ffn_w13[l, 1], ffn_w2[l, 1]),
                       ln_g[l, 2], ln_b[l, 2])
    return x


def setup_inputs(seed: int = 0) -> dict:
    key = jax.random.key(seed)
    ks = jax.random.split(key, 16)
    f32 = jnp.float32
    nrm = lambda k, shp, s: jax.random.normal(k, shp, f32) * s
    x_prompt = jax.random.normal(ks[0], (BATCH, SEQ, D_MODEL), f32)
    x_sample = jax.random.normal(ks[1], (DEC_BATCH, DEC_SEQ, D_MODEL), f32)
    ln_g = 1.0 + nrm(ks[2], (DEPTH, 3, D_MODEL), 0.02)
    ln_b = nrm(ks[3], (DEPTH, 3, D_MODEL), 0.02)
    ffn_w13 = nrm(ks[4], (DEPTH, 2, D_MODEL, 2 * D_FF), D_MODEL ** -0.5)
    ffn_w2 = nrm(ks[5], (DEPTH, 2, D_FF, D_MODEL), BETA_INIT * D_FF ** -0.5)
    col_scale = jnp.concatenate([
        jnp.ones((ATTN_WIDTH + KV_WIDTH,), f32),
        jnp.full((KV_WIDTH,), BETA_INIT, f32),
        jnp.ones((3 * HG_KDIM,), f32),
        jnp.full((HG_WIDTH,), BETA_INIT, f32),
        jnp.ones((HG_WIDTH,), f32)])
    w_in = nrm(ks[6], (DEPTH, D_MODEL, IN_COLS), D_MODEL ** -0.5) * col_scale
    attn_sink = nrm(ks[7], (DEPTH, N_HEADS), 0.5)
    attn_norm_g = 1.0 + nrm(ks[8], (DEPTH, ATTN_WIDTH), 0.02)
    hg_lb_fwd = nrm(ks[9], (DEPTH + 1, HG_KDIM), 0.5)
    hg_lb_bwd = nrm(ks[10], (DEPTH + 1, HG_KDIM), 0.5)
    hg_norm_g = 1.0 + nrm(ks[11], (DEPTH, HG_WIDTH), 0.02)
    w_out = nrm(ks[12], (DEPTH, MIX_WIDTH, D_MODEL), BETA_INIT * MIX_WIDTH ** -0.5)
    return {"x_prompt": x_prompt, "x_sample": x_sample, "ln_g": ln_g, "ln_b": ln_b,
            "ffn_w13": ffn_w13, "ffn_w2": ffn_w2, "w_in": w_in, "attn_sink": attn_sink,
            "attn_norm_g": attn_norm_g, "hg_lb_fwd": hg_lb_fwd, "hg_lb_bwd": hg_lb_bwd,
            "hg_norm_g": hg_norm_g, "w_out": w_out}


def reference(x_prompt, x_sample, ln_g, ln_b, ffn_w13, ffn_w2, w_in, attn_sink,
              attn_norm_g, hg_lb_fwd, hg_lb_bwd, hg_norm_g, w_out):
    y_prompt = trunk(x_prompt, ln_g, ln_b, ffn_w13, ffn_w2, w_in, attn_sink, attn_norm_g,
                     hg_lb_fwd, hg_lb_bwd, hg_norm_g, w_out)
    y_sample = trunk(x_sample, ln_g, ln_b, ffn_w13, ffn_w2, w_in, attn_sink, attn_norm_g,
                     hg_lb_fwd, hg_lb_bwd, hg_norm_g, w_out)
    return (y_prompt, y_sample)
```

```python
import functools

import jax
import jax.numpy as jnp
from jax import lax
from jax.experimental import pallas as pl
from jax.experimental.pallas import tpu as pltpu

F32 = jnp.float32
BF16 = jnp.bfloat16

D_MODEL = 1024
DEPTH = 1
ATTN_WIDTH = 512
HEAD_DIM = 64
N_HEADS = 8
N_KV_HEADS = 2
KV_WIDTH = N_KV_HEADS * HEAD_DIM
WINDOW = 128
HG_WIDTH = 512
HG_HEADS = 4
HG_DK = 128
D_FF = 2816
ALPHA = (2.0 * DEPTH) ** 0.25
EPS = 1e-5
NEG = -1e30

LANES = 128
VMEM_LIMIT = 56 * 1024 * 1024

ROW_TILE = 256
ATTN_TILE = 512
GLA_TILE = 512
GLA_CHUNK = 128
GLA_BLK = 16
GLA_SUB = 4


def _dot(a, b):
    return jnp.dot(a, b, preferred_element_type=F32)


def _dot_nt(a, b):
    return lax.dot_general(a, b, (((1,), (1,)), ((), ())), preferred_element_type=F32)


def _dot_tn(a, b):
    return lax.dot_general(a, b, (((0,), (0,)), ((), ())), preferred_element_type=F32)


def _layer_norm(y, g, b):
    mu = jnp.mean(y, axis=-1, keepdims=True)
    yc = y - mu
    var = jnp.mean(yc * yc, axis=-1, keepdims=True)
    return yc * lax.rsqrt(var + EPS) * g + b


def _swiglu(xb, w13_ref, w2_ref):
    h = _dot(xb, w13_ref[...])
    gate = h[:, :D_FF]
    up = h[:, D_FF:]
    act = (gate * jax.nn.sigmoid(gate)) * up
    return _dot(act.astype(BF16), w2_ref[...])


def _pre_kernel(x_ref, w13_ref, w2_ref, lng_ref, lnb_ref, win_ref,
                x1_ref, qa_ref, kv_ref, qh_ref, ih_ref, gh_ref, ff_ref, fb_ref):
    x = x_ref[...]
    ffn = _swiglu(x.astype(BF16), w13_ref, w2_ref)
    x1 = _layer_norm(ALPHA * x + 0.5 * ffn, lng_ref[...], lnb_ref[...])
    x1_ref[...] = x1
    proj = _dot(x1.astype(BF16), win_ref[...])
    off = 0
    for ref in (qa_ref, kv_ref, qh_ref, ih_ref, gh_ref, ff_ref, fb_ref):
        w = ref.shape[-1]
        ref[...] = proj[:, off:off + w].astype(ref.dtype)
        off += w


def _resident(shape):
    nd = len(shape)
    return pl.BlockSpec(shape, lambda *_: (0,) * nd, pipeline_mode=pl.Buffered(1))


def _rows(tm, width):
    return pl.BlockSpec((tm, width), lambda i: (i, 0))


def _pre_call(x, w13, w2, lng, lnb, win, tm):
    m = x.shape[0]
    widths = (ATTN_WIDTH, 2 * KV_WIDTH, HG_WIDTH, HG_WIDTH, HG_WIDTH, HG_WIDTH, HG_WIDTH)
    dtypes = (BF16, BF16, BF16, BF16, BF16, F32, F32)
    out_shape = [jax.ShapeDtypeStruct((m, D_MODEL), F32)]
    out_shape += [jax.ShapeDtypeStruct((m, w), d) for w, d in zip(widths, dtypes)]
    out_specs = [_rows(tm, D_MODEL)] + [_rows(tm, w) for w in widths]
    return pl.pallas_call(
        _pre_kernel,
        grid=(m // tm,),
        in_specs=[_rows(tm, D_MODEL), _resident(w13.shape), _resident(w2.shape),
                  _resident(lng.shape), _resident(lnb.shape), _resident(win.shape)],
        out_specs=out_specs,
        out_shape=out_shape,
        compiler_params=pltpu.CompilerParams(
            dimension_semantics=("arbitrary",), vmem_limit_bytes=VMEM_LIMIT),
        name="pre",
    )(x, w13, w2, lng, lnb, win)


def _attn_kernel(sink_ref, q_ref, kvp_ref, kvc_ref, kvn_ref, ng_ref, o_ref,
                 kx_ref, vx_ref, bias_ref, *, tq):
    b_id = pl.program_id(0)
    i_id = pl.program_id(1)
    n_sub = tq // WINDOW
    span = 3 * WINDOW

    @pl.when((b_id == 0) & (i_id == 0))
    def _():
        r = lax.broadcasted_iota(jnp.int32, (WINDOW, span), 0)
        c = lax.broadcasted_iota(jnp.int32, (WINDOW, span), 1)
        dist = jnp.abs(c - WINDOW - r).astype(F32)
        for h in range(N_HEADS):
            slope = 2.0 ** (-8.0 * (h + 1) / N_HEADS)
            bias_ref[h] = jnp.where(dist <= float(WINDOW), -slope * dist, NEG)

    for lo_row, ref, n in ((0, kvp_ref, WINDOW), (WINDOW, kvc_ref, tq),
                           (WINDOW + tq, kvn_ref, WINDOW)):
        kv = ref[0]
        low = lax.broadcasted_iota(jnp.int32, (n, LANES), 1) < HEAD_DIM
        zero = jnp.zeros((n, LANES), BF16)
        for x, dst in ((kv[:, :LANES], kx_ref), (kv[:, LANES:], vx_ref)):
            xr = jnp.concatenate([x[:, HEAD_DIM:], x[:, :HEAD_DIM]], axis=1)
            rows = pl.ds(lo_row, n)
            dst[0, rows, :] = jnp.where(low, x, zero)
            dst[1, rows, :] = jnp.where(low, zero, xr)
            dst[2, rows, :] = jnp.where(low, xr, zero)
            dst[3, rows, :] = jnp.where(low, zero, x)

    col = lax.broadcasted_iota(jnp.int32, (WINDOW, span), 1)
    pen_lo = jnp.where(i_id == 0, NEG, 0.0).astype(F32)
    pen_hi = jnp.where(i_id == pl.num_programs(1) - 1, NEG, 0.0).astype(F32)
    low_out = lax.broadcasted_iota(jnp.int32, (WINDOW, LANES), 1) < HEAD_DIM
    gain = ng_ref[...]

    for j in range(n_sub):
        rows = slice(j * WINDOW, (j + 1) * WINDOW)
        srows = slice(j * WINDOW, j * WINDOW + span)
        outs = []
        for g in range(N_KV_HEADS):
            qs = jnp.concatenate(
                [q_ref[0, rows, g * 256:g * 256 + LANES],
                 q_ref[0, rows, g * 256 + LANES:(g + 1) * 256]], axis=0)
            logit = [_dot_nt(qs, kx_ref[2 * g + e, srows, :]) for e in range(2)]
            for r in range(2):
                acc = None
                dens = []
                for e in range(2):
                    h = 4 * g + 2 * r + e
                    l = logit[e][r * WINDOW:(r + 1) * WINDOW, :] + bias_ref[h]
                    if j == 0:
                        l = l + jnp.where(col < WINDOW, pen_lo, 0.0)
                    if j == n_sub - 1:
                        l = l + jnp.where(col >= 2 * WINDOW, pen_hi, 0.0)
                    sk = sink_ref[h]
                    m = jnp.maximum(jnp.max(l, axis=1, keepdims=True), sk)
                    p = jnp.exp(l - m)
                    dens.append(jnp.sum(p, axis=1, keepdims=True) + jnp.exp(sk - m))
                    pv = _dot(p.astype(BF16), vx_ref[2 * g + e, srows, :])
                    acc = pv if acc is None else acc + pv
                inv = jnp.where(low_out, 1.0 / dens[0], 1.0 / dens[1])
                outs.append(acc * inv)
        oj = jnp.concatenate(outs, axis=1)
        ms = jnp.mean(oj * oj, axis=1, keepdims=True)
        o_ref[0, rows, :] = (oj * lax.rsqrt(ms + EPS) * gain).astype(o_ref.dtype)


def _attn_call(sink, qa, kv, ng, tq):
    bsz, t, _ = qa.shape
    r = tq // WINDOW
    nblk = t // WINDOW
    kern = functools.partial(_attn_kernel, tq=tq)
    return pl.pallas_call(
        kern,
        grid=(bsz, t // tq),
        in_specs=[
            pl.BlockSpec(memory_space=pltpu.SMEM),
            pl.BlockSpec((1, tq, ATTN_WIDTH), lambda b, i: (b, i, 0)),
            pl.BlockSpec((1, WINDOW, 2 * KV_WIDTH), lambda b, i: (b, jnp.maximum(i * r - 1, 0), 0)),
            pl.BlockSpec((1, tq, 2 * KV_WIDTH), lambda b, i: (b, i, 0)),
            pl.BlockSpec((1, WINDOW, 2 * KV_WIDTH),
                         lambda b, i: (b, jnp.minimum((i + 1) * r, nblk - 1), 0)),
            pl.BlockSpec((1, ATTN_WIDTH), lambda b, i: (0, 0)),
        ],
        out_specs=pl.BlockSpec((1, tq, ATTN_WIDTH), lambda b, i: (b, i, 0)),
        out_shape=jax.ShapeDtypeStruct((bsz, t, ATTN_WIDTH), BF16),
        scratch_shapes=[
            pltpu.VMEM((4, tq + 2 * WINDOW, LANES), BF16),
            pltpu.VMEM((4, tq + 2 * WINDOW, LANES), BF16),
            pltpu.VMEM((N_HEADS, WINDOW, 3 * WINDOW), F32),
        ],
        compiler_params=pltpu.CompilerParams(
            dimension_semantics=("arbitrary", "arbitrary"), vmem_limit_bytes=VMEM_LIMIT),
        name="attn",
    )(sink, qa, kv, kv, kv, ng)


def _gla_consts(c, rev):
    ti = lax.broadcasted_iota(jnp.int32, (c, c), 0)
    si = lax.broadcasted_iota(jnp.int32, (c, c), 1)
    d = (si - ti) if rev else (ti - si)
    cum = jnp.where(d >= 0, 1.0, 0.0).astype(BF16)
    same_blk = (ti // GLA_BLK) == (si // GLA_BLK)
    same_sub = (ti // GLA_SUB) == (si // GLA_SUB)
    code = jnp.where(same_sub, d, -1)
    row = lax.broadcasted_iota(jnp.int32, (c, LANES), 0)
    return dict(cum=cum, same_blk=same_blk, code=code,
                blk=row // GLA_BLK, sub=(row // GLA_SUB) % (GLA_BLK // GLA_SUB))


def _bcast_rows(b, rows, n):
    return jnp.concatenate([jnp.broadcast_to(b[r:r + 1, :], (n, LANES)) for r in rows], axis=0)


def _gla_chunk(q, v, flog, lb, st, cst, rev):
    c = q.shape[0]
    nb = c // GLA_BLK
    ns = GLA_BLK // GLA_SUB
    qf = q.astype(F32)
    s = jax.nn.sigmoid(flog)
    g = jnp.log(lb + (1.0 - lb) * s)
    k = (1.0 - lb) * (1.0 - s)

    g1 = g.astype(BF16)
    r1 = g - g1.astype(F32)
    g2 = r1.astype(BF16)
    g3 = (r1 - g2.astype(F32)).astype(BF16)
    b3 = _dot(cst["cum"], jnp.concatenate([g1, g2, g3], axis=1))
    b = (b3[:, :LANES] + b3[:, LANES:2 * LANES]) + b3[:, 2 * LANES:]

    blk, sub = cst["blk"], cst["sub"]
    blk_end = [GLA_BLK * j + (0 if rev else GLA_BLK - 1) for j in range(nb)]
    sub_end = [[GLA_BLK * j + GLA_SUB * u + (0 if rev else GLA_SUB - 1) for j in range(nb)]
               for u in range(ns)]

    src_blocks = range(1, nb) if rev else range(nb - 1)
    e_blk = _bcast_rows(b, blk_end, GLA_BLK)
    kt = k * jnp.exp(e_blk - b)
    lhs, rhs = [], []
    for j in src_blocks:
        later = (blk < j) if rev else (blk > j)
        e = b[blk_end[j]:blk_end[j] + 1, :]
        lhs.append((qf * jnp.exp(jnp.where(later, b - e, NEG))).astype(BF16))
        rhs.append(jnp.where(blk == j, kt, 0.0).astype(BF16))
    a1 = _dot_nt(jnp.concatenate(lhs, axis=1), jnp.concatenate(rhs, axis=1))

    e_sub = [_bcast_rows(b, sub_end[u], GLA_BLK) for u in range(ns)]
    e_own = e_sub[ns - 1]
    for u in range(ns - 2, -1, -1):
        e_own = jnp.where(sub == u, e_sub[u], e_own)
    kt2 = k * jnp.exp(e_own - b)
    src_subs = range(1, ns) if rev else range(ns - 1)
    lhs, rhs = [], []
    for u in src_subs:
        later = (sub < u) if rev else (sub > u)
        lhs.append((qf * jnp.exp(jnp.where(later, b - e_sub[u], NEG))).astype(BF16))
        rhs.append(jnp.where(sub == u, kt2, 0.0).astype(BF16))
    a2 = _dot_nt(jnp.concatenate(lhs, axis=1), jnp.concatenate(rhs, axis=1))

    a3 = jnp.zeros((c, c), F32)
    for dist in range(GLA_SUB - 1, -1, -1):
        if dist == 0:
            rs = jnp.sum(qf * k, axis=1, keepdims=True)
        else:
            sh = (c - dist) if rev else dist
            ks = pltpu.roll(k, sh, 0)
            bs = pltpu.roll(b, sh, 0)
            rs = jnp.sum(qf * ks * jnp.exp(jnp.minimum(b - bs, 0.0)), axis=1, keepdims=True)
        a3 = jnp.where(cst["code"] == dist, rs, a3)

    a = a1 + jnp.where(cst["same_blk"], a2, 0.0) + a3

    e_end = b[0:1, :] if rev else b[c - 1:c, :]
    qd = (qf * jnp.exp(b)).astype(BF16)
    o = _dot(a.astype(BF16), v) + _dot_nt(qd, st.astype(BF16))
    kdec = (k * jnp.exp(e_end - b)).astype(BF16)
    st_new = st * jnp.exp(e_end) + _dot_tn(v, kdec)
    return o, st_new


def _lower_bound(lb_ref, h):
    p = lb_ref[:, h * LANES:(h + 1) * LANES]
    e = jnp.exp(p - jnp.max(p, axis=0, keepdims=True))
    return e[0:1, :] / jnp.sum(e, axis=0, keepdims=True)


def _gla_kernel(qf_ref, vf_ref, ff_ref, qb_ref, vb_ref, fb_ref, lbf_ref, lbb_ref,
                of_ref, ob_ref, st_ref, *, tb, chunk):
    n_chunks = tb // chunk

    @pl.when(pl.program_id(1) == 0)
    def _():
        st_ref[...] = jnp.zeros(st_ref.shape, F32)

    dirs = (
        (False, qf_ref, vf_ref, ff_ref, lbf_ref, of_ref, _gla_consts(chunk, False)),
        (True, qb_ref, vb_ref, fb_ref, lbb_ref, ob_ref, _gla_consts(chunk, True)),
    )
    lbs = [[_lower_bound(d[4], h) for h in range(HG_HEADS)] for d in dirs]

    def body(ci, carry):
        for di, (rev, q_ref, v_ref, f_ref, _, o_ref, cst) in enumerate(dirs):
            cidx = (n_chunks - 1 - ci) if rev else ci
            rows = pl.ds(pl.multiple_of(cidx * chunk, chunk), chunk)
            for h in range(HG_HEADS):
                cols = slice(h * LANES, (h + 1) * LANES)
                o, st_new = _gla_chunk(q_ref[0, rows, cols], v_ref[0, rows, cols],
                                       f_ref[0, rows, cols], lbs[di][h],
                                       st_ref[di, h], cst, rev)
                o_ref[0, rows, cols] = o
                st_ref[di, h] = st_new
        return carry

    lax.fori_loop(0, n_chunks, body, 0)


def _gla_call(qh, ih, ff, fb, lbf, lbb, tb, chunk):
    bsz, t, _ = qh.shape
    nt = t // tb
    fwd = pl.BlockSpec((1, tb, HG_WIDTH), lambda b, i: (b, i, 0))
    bwd = pl.BlockSpec((1, tb, HG_WIDTH), lambda b, i: (b, nt - 1 - i, 0))
    lbs = pl.BlockSpec(lbf.shape, lambda b, i: (0, 0))
    kern = functools.partial(_gla_kernel, tb=tb, chunk=chunk)
    return pl.pallas_call(
        kern,
        grid=(bsz, nt),
        in_specs=[fwd, fwd, fwd, bwd, bwd, bwd, lbs, lbs],
        out_specs=[fwd, bwd],
        out_shape=[jax.ShapeDtypeStruct((bsz, t, HG_WIDTH), F32)] * 2,
        scratch_shapes=[pltpu.VMEM((2, HG_HEADS, LANES, HG_DK), F32)],
        compiler_params=pltpu.CompilerParams(
            dimension_semantics=("arbitrary", "arbitrary"), vmem_limit_bytes=VMEM_LIMIT),
        name="gla",
    )(qh, ih, ff, qh, ih, fb, lbf, lbb)


def _post_kernel(x1_ref, oa_ref, of_ref, ob_ref, gh_ref, hg_ref, wout_ref,
                 ln2g_ref, ln2b_ref, w13_ref, w2_ref, ln3g_ref, ln3b_ref, out_ref):
    o = of_ref[...] + ob_ref[...]
    parts = []
    for h in range(HG_HEADS):
        oh = o[:, h * LANES:(h + 1) * LANES]
        ms = jnp.mean(oh * oh, axis=1, keepdims=True)
        parts.append(oh * lax.rsqrt(ms + EPS))
    gate = gh_ref[...].astype(F32)
    ohg = jnp.concatenate(parts, axis=1) * hg_ref[...] * (gate * jax.nn.sigmoid(gate))
    mix = (_dot(oa_ref[...], wout_ref[:ATTN_WIDTH, :])
           + _dot(ohg.astype(BF16), wout_ref[ATTN_WIDTH:, :]))
    x2 = _layer_norm(ALPHA * x1_ref[...] + mix, ln2g_ref[...], ln2b_ref[...])
    ffn = _swiglu(x2.astype(BF16), w13_ref, w2_ref)
    out_ref[...] = _layer_norm(ALPHA * x2 + 0.5 * ffn, ln3g_ref[...], ln3b_ref[...])


def _post_call(x1, oa, of, ob, gh, hg, wout, ln2g, ln2b, w13, w2, ln3g, ln3b, tm):
    m = x1.shape[0]
    return pl.pallas_call(
        _post_kernel,
        grid=(m // tm,),
        in_specs=[_rows(tm, D_MODEL), _rows(tm, ATTN_WIDTH), _rows(tm, HG_WIDTH),
                  _rows(tm, HG_WIDTH), _rows(tm, HG_WIDTH), _resident(hg.shape),
                  _resident(wout.shape), _resident(ln2g.shape), _resident(ln2b.shape),
                  _resident(w13.shape), _resident(w2.shape), _resident(ln3g.shape),
                  _resident(ln3b.shape)],
        out_specs=_rows(tm, D_MODEL),
        out_shape=jax.ShapeDtypeStruct((m, D_MODEL), F32),
        compiler_params=pltpu.CompilerParams(
            dimension_semantics=("arbitrary",), vmem_limit_bytes=VMEM_LIMIT),
        name="post",
    )(x1, oa, of, ob, gh, hg, wout, ln2g, ln2b, w13, w2, ln3g, ln3b)


def _prep_weights(ln_g, ln_b, ffn_w13, ffn_w2, w_in, attn_sink, attn_norm_g,
                  hg_lb_fwd, hg_lb_bwd, hg_norm_g, w_out):
    l = 0
    sizes = (ATTN_WIDTH, KV_WIDTH, KV_WIDTH, HG_WIDTH, HG_WIDTH, HG_WIDTH, HG_WIDTH, HG_WIDTH)
    offs = [0]
    for s in sizes:
        offs.append(offs[-1] + s)
    cols = [w_in[l][:, offs[i]:offs[i + 1]] for i in range(len(sizes))]
    q_a, k_a, v_a, q_h, f_f, f_b, i_h, g_h = cols
    win = jnp.concatenate([q_a * HEAD_DIM ** -0.5, k_a, v_a, q_h, i_h, g_h, f_f, f_b], axis=1)
    row = lambda a: a.reshape(1, -1).astype(F32)
    return dict(
        w13=[ffn_w13[l, i].astype(BF16) for i in range(2)],
        w2=[ffn_w2[l, i].astype(BF16) for i in range(2)],
        lng=[row(ln_g[l, i]) for i in range(3)],
        lnb=[row(ln_b[l, i]) for i in range(3)],
        win=win.astype(BF16),
        sink=attn_sink[l].astype(F32),
        ng=row(attn_norm_g[l]),
        lbf=hg_lb_fwd.astype(F32),
        lbb=hg_lb_bwd.astype(F32),
        hg=row(hg_norm_g[l]),
        wout=w_out[l].astype(BF16),
    )


def _largest_tile(n, cap, unit):
    t = min(cap, n)
    while n % t or t % unit:
        t -= unit
    return t


def _layer(x, w):
    bsz, t, d = x.shape
    m = bsz * t
    tm = _largest_tile(m, ROW_TILE, 8)
    tq = _largest_tile(t, ATTN_TILE, WINDOW)
    tb = _largest_tile(t, GLA_TILE, GLA_CHUNK)
    x1, qa, kv, qh, ih, gh, ff, fb = _pre_call(
        x.reshape(m, d), w["w13"][0], w["w2"][0], w["lng"][0], w["lnb"][0], w["win"], tm)
    seq = lambda a: a.reshape(bsz, t, a.shape[-1])
    oa = _attn_call(w["sink"], seq(qa), seq(kv), w["ng"], tq)
    of, ob = _gla_call(seq(qh), seq(ih), seq(ff), seq(fb), w["lbf"], w["lbb"], tb, GLA_CHUNK)
    flat = lambda a: a.reshape(m, a.shape[-1])
    y = _post_call(x1, flat(oa), flat(of), flat(ob), gh, w["hg"], w["wout"],
                   w["lng"][1], w["lnb"][1], w["w13"][1], w["w2"][1],
                   w["lng"][2], w["lnb"][2], tm)
    return y.reshape(bsz, t, d)


def kernel(x_prompt, x_sample, ln_g, ln_b, ffn_w13, ffn_w2, w_in, attn_sink, attn_norm_g,
           hg_lb_fwd, hg_lb_bwd, hg_norm_g, w_out):
    w = _prep_weights(ln_g, ln_b, ffn_w13, ffn_w2, w_in, attn_sink, attn_norm_g,
                      hg_lb_fwd, hg_lb_bwd, hg_norm_g, w_out)
    return (_layer(x_prompt, w), _layer(x_sample, w))
```

```python
import functools

import jax
import jax.numpy as jnp
from jax import lax
from jax.experimental import pallas as pl
from jax.experimental.pallas import tpu as pltpu

F32 = jnp.float32
BF16 = jnp.bfloat16

D_MODEL = 1024
DEPTH = 1
ATTN_WIDTH = 512
HEAD_DIM = 64
N_HEADS = 8
N_KV_HEADS = 2
KV_WIDTH = N_KV_HEADS * HEAD_DIM
WINDOW = 128
HG_WIDTH = 512
HG_HEADS = 4
HG_DK = 128
D_FF = 2816
ALPHA = (2.0 * DEPTH) ** 0.25
EPS = 1e-5
NEG = -1e30

LANES = 128
VMEM_LIMIT = 56 * 1024 * 1024

ROW_TILE = 256
ATTN_TILE = 512
GLA_TILE = 512
GLA_CHUNK = 128
GLA_BLK = 16
GLA_SUB = 4


def _dot(a, b):
    return jnp.dot(a, b, preferred_element_type=F32)


def _dot_nt(a, b):
    return lax.dot_general(a, b, (((1,), (1,)), ((), ())), preferred_element_type=F32)


def _dot_tn(a, b):
    return lax.dot_general(a, b, (((0,), (0,)), ((), ())), preferred_element_type=F32)


def _layer_norm(y, g, b):
    mu = jnp.mean(y, axis=-1, keepdims=True)
    yc = y - mu
    var = jnp.mean(yc * yc, axis=-1, keepdims=True)
    return yc * lax.rsqrt(var + EPS) * g + b


def _swiglu(xb, w13_ref, w2_ref):
    h = _dot(xb, w13_ref[...])
    gate = h[:, :D_FF]
    up = h[:, D_FF:]
    act = (gate * jax.nn.sigmoid(gate)) * up
    return _dot(act.astype(BF16), w2_ref[...])


def _pre_kernel(x_ref, w13_ref, w2_ref, lng_ref, lnb_ref, win_ref,
                x1_ref, qa_ref, kv_ref, qh_ref, ih_ref, gh_ref, ff_ref, fb_ref):
    x = x_ref[...]
    ffn = _swiglu(x.astype(BF16), w13_ref, w2_ref)
    x1 = _layer_norm(ALPHA * x + 0.5 * ffn, lng_ref[...], lnb_ref[...])
    x1_ref[...] = x1
    proj = _dot(x1.astype(BF16), win_ref[...])
    off = 0
    for ref in (qa_ref, kv_ref, qh_ref, ih_ref, gh_ref, ff_ref, fb_ref):
        w = ref.shape[-1]
        ref[...] = proj[:, off:off + w].astype(ref.dtype)
        off += w


def _resident(shape):
    nd = len(shape)
    return pl.BlockSpec(shape, lambda *_: (0,) * nd, pipeline_mode=pl.Buffered(1))


def _rows(tm, width):
    return pl.BlockSpec((tm, width), lambda i: (i, 0))


def _pre_call(x, w13, w2, lng, lnb, win, tm):
    m = x.shape[0]
    widths = (ATTN_WIDTH, 2 * KV_WIDTH, HG_WIDTH, HG_WIDTH, HG_WIDTH, HG_WIDTH, HG_WIDTH)
    dtypes = (BF16, BF16, BF16, BF16, BF16, F32, F32)
    out_shape = [jax.ShapeDtypeStruct((m, D_MODEL), F32)]
    out_shape += [jax.ShapeDtypeStruct((m, w), d) for w, d in zip(widths, dtypes)]
    out_specs = [_rows(tm, D_MODEL)] + [_rows(tm, w) for w in widths]
    return pl.pallas_call(
        _pre_kernel,
        grid=(m // tm,),
        in_specs=[_rows(tm, D_MODEL), _resident(w13.shape), _resident(w2.shape),
                  _resident(lng.shape), _resident(lnb.shape), _resident(win.shape)],
        out_specs=out_specs,
        out_shape=out_shape,
        compiler_params=pltpu.CompilerParams(
            dimension_semantics=("arbitrary",), vmem_limit_bytes=VMEM_LIMIT),
        name="pre",
    )(x, w13, w2, lng, lnb, win)


def _attn_kernel(sink_ref, q_ref, kvp_ref, kvc_ref, kvn_ref, ng_ref, o_ref,
                 kx_ref, vx_ref, bias_ref, *, tq):
    b_id = pl.program_id(0)
    i_id = pl.program_id(1)
    n_sub = tq // WINDOW
    span = 3 * WINDOW

    @pl.when((b_id == 0) & (i_id == 0))
    def _():
        r = lax.broadcasted_iota(jnp.int32, (WINDOW, span), 0)
        c = lax.broadcasted_iota(jnp.int32, (WINDOW, span), 1)
        dist = jnp.abs(c - WINDOW - r).astype(F32)
        for h in range(N_HEADS):
            slope = 2.0 ** (-8.0 * (h + 1) / N_HEADS)
            bias_ref[h] = jnp.where(dist <= float(WINDOW), -slope * dist, NEG)

    for lo_row, ref, n in ((0, kvp_ref, WINDOW), (WINDOW, kvc_ref, tq),
                           (WINDOW + tq, kvn_ref, WINDOW)):
        kv = ref[0]
        low = lax.broadcasted_iota(jnp.int32, (n, LANES), 1) < HEAD_DIM
        zero = jnp.zeros((n, LANES), BF16)
        for x, dst in ((kv[:, :LANES], kx_ref), (kv[:, LANES:], vx_ref)):
            xr = jnp.concatenate([x[:, HEAD_DIM:], x[:, :HEAD_DIM]], axis=1)
            rows = pl.ds(lo_row, n)
            dst[0, rows, :] = jnp.where(low, x, zero)
            dst[1, rows, :] = jnp.where(low, zero, xr)
            dst[2, rows, :] = jnp.where(low, xr, zero)
            dst[3, rows, :] = jnp.where(low, zero, x)

    col = lax.broadcasted_iota(jnp.int32, (WINDOW, span), 1)
    pen_lo = jnp.where(i_id == 0, NEG, 0.0).astype(F32)
    pen_hi = jnp.where(i_id == pl.num_programs(1) - 1, NEG, 0.0).astype(F32)
    low_out = lax.broadcasted_iota(jnp.int32, (WINDOW, LANES), 1) < HEAD_DIM
    gain = ng_ref[...]

    for j in range(n_sub):
        rows = slice(j * WINDOW, (j + 1) * WINDOW)
        srows = slice(j * WINDOW, j * WINDOW + span)
        outs = []
        for g in range(N_KV_HEADS):
            qs = jnp.concatenate(
                [q_ref[0, rows, g * 256:g * 256 + LANES],
                 q_ref[0, rows, g * 256 + LANES:(g + 1) * 256]], axis=0)
            logit = [_dot_nt(qs, kx_ref[2 * g + e, srows, :]) for e in range(2)]
            for r in range(2):
                acc = None
                dens = []
                for e in range(2):
                    h = 4 * g + 2 * r + e
                    l = logit[e][r * WINDOW:(r + 1) * WINDOW, :] + bias_ref[h]
                    if j == 0:
                        l = l + jnp.where(col < WINDOW, pen_lo, 0.0)
                    if j == n_sub - 1:
                        l = l + jnp.where(col >= 2 * WINDOW, pen_hi, 0.0)
                    sk = sink_ref[h]
                    m = jnp.maximum(jnp.max(l, axis=1, keepdims=True), sk)
                    p = jnp.exp(l - m)
                    dens.append(jnp.sum(p, axis=1, keepdims=True) + jnp.exp(sk - m))
                    pv = _dot(p.astype(BF16), vx_ref[2 * g + e, srows, :])
                    acc = pv if acc is None else acc + pv
                inv = jnp.where(low_out, 1.0 / dens[0], 1.0 / dens[1])
                outs.append(acc * inv)
        oj = jnp.concatenate(outs, axis=1)
        ms = jnp.mean(oj * oj, axis=1, keepdims=True)
        o_ref[0, rows, :] = (oj * lax.rsqrt(ms + EPS) * gain).astype(o_ref.dtype)


def _attn_call(sink, qa, kv, ng, tq):
    bsz, t, _ = qa.shape
    r = tq // WINDOW
    nblk = t // WINDOW
    kern = functools.partial(_attn_kernel, tq=tq)
    return pl.pallas_call(
        kern,
        grid=(bsz, t // tq),
        in_specs=[
            pl.BlockSpec(memory_space=pltpu.SMEM),
            pl.BlockSpec((1, tq, ATTN_WIDTH), lambda b, i: (b, i, 0)),
            pl.BlockSpec((1, WINDOW, 2 * KV_WIDTH), lambda b, i: (b, jnp.maximum(i * r - 1, 0), 0)),
            pl.BlockSpec((1, tq, 2 * KV_WIDTH), lambda b, i: (b, i, 0)),
            pl.BlockSpec((1, WINDOW, 2 * KV_WIDTH),
                         lambda b, i: (b, jnp.minimum((i + 1) * r, nblk - 1), 0)),
            pl.BlockSpec((1, ATTN_WIDTH), lambda b, i: (0, 0)),
        ],
        out_specs=pl.BlockSpec((1, tq, ATTN_WIDTH), lambda b, i: (b, i, 0)),
        out_shape=jax.ShapeDtypeStruct((bsz, t, ATTN_WIDTH), BF16),
        scratch_shapes=[
            pltpu.VMEM((4, tq + 2 * WINDOW, LANES), BF16),
            pltpu.VMEM((4, tq + 2 * WINDOW, LANES), BF16),
            pltpu.VMEM((N_HEADS, WINDOW, 3 * WINDOW), F32),
        ],
        compiler_params=pltpu.CompilerParams(
            dimension_semantics=("arbitrary", "arbitrary"), vmem_limit_bytes=VMEM_LIMIT),
        name="attn",
    )(sink, qa, kv, kv, kv, ng)


def _gla_consts(c, rev):
    ti = lax.broadcasted_iota(jnp.int32, (c, c), 0)
    si = lax.broadcasted_iota(jnp.int32, (c, c), 1)
    d = (si - ti) if rev else (ti - si)
    cum = jnp.where(d >= 0, 1.0, 0.0).astype(BF16)
    same_blk = (ti // GLA_BLK) == (si // GLA_BLK)
    same_sub = (ti // GLA_SUB) == (si // GLA_SUB)
    code = jnp.where(same_sub, d, -1)
    row = lax.broadcasted_iota(jnp.int32, (c, HG_WIDTH), 0)
    return dict(cum=cum, same_blk=same_blk, code=code,
                sub=(row // GLA_SUB) % (GLA_BLK // GLA_SUB))


def _bcast_rows(b, rows, n):
    return jnp.concatenate(
        [jnp.broadcast_to(b[r:r + 1, :], (n, b.shape[1])) for r in rows], axis=0)


def _stack_rows(parts):
    return jnp.concatenate([p for p in parts if p.shape[0]], axis=0)


def _head(x, h):
    return x[:, h * LANES:(h + 1) * LANES]


def _gla_chunk_steps(q, v, flog, lb, state, emit, cst, rev):
    c, width = flog.shape
    nb = c // GLA_BLK
    ns = GLA_BLK // GLA_SUB
    heads = range(width // LANES)
    qf = q.astype(F32)
    s = jax.nn.sigmoid(flog)
    g = jnp.log(lb + (1.0 - lb) * s)
    k = (1.0 - lb) * (1.0 - s)
    g1 = g.astype(BF16)
    r1 = g - g1.astype(F32)
    g2 = r1.astype(BF16)
    g3 = (r1 - g2.astype(F32)).astype(BF16)
    gsplit = jnp.concatenate([g1, g2, g3], axis=1)
    yield

    b3 = _dot(cst["cum"], gsplit)
    b = (b3[:, :width] + b3[:, width:2 * width]) + b3[:, 2 * width:]
    sub = cst["sub"]
    blk_end = [GLA_BLK * j + (0 if rev else GLA_BLK - 1) for j in range(nb)]
    sub_end = [[GLA_BLK * j + GLA_SUB * u + (0 if rev else GLA_SUB - 1) for j in range(nb)]
               for u in range(ns)]

    src_blocks = range(1, nb) if rev else range(nb - 1)
    e_blk = _bcast_rows(b, blk_end, GLA_BLK)
    kt = (k * jnp.exp(e_blk - b)).astype(BF16)
    lhs1, rhs1 = [], []
    for j in src_blocks:
        lo, hi = GLA_BLK * j, GLA_BLK * (j + 1)
        later = slice(0, lo) if rev else slice(hi, c)
        e = b[blk_end[j]:blk_end[j] + 1, :]
        part = (qf[later] * jnp.exp(b[later] - e)).astype(BF16)
        pad = jnp.zeros((c - part.shape[0], width), BF16)
        lhs1.append(jnp.concatenate([part, pad] if rev else [pad, part], axis=0))
        rhs1.append(_stack_rows([jnp.zeros((lo, width), BF16), kt[lo:hi],
                                 jnp.zeros((c - hi, width), BF16)]))

    e_sub = [_bcast_rows(b, sub_end[u], GLA_BLK) for u in range(ns)]
    e_own = e_sub[ns - 1]
    for u in range(ns - 2, -1, -1):
        e_own = jnp.where(sub == u, e_sub[u], e_own)
    kt2 = k * jnp.exp(e_own - b)
    src_subs = range(1, ns) if rev else range(ns - 1)
    lhs2, rhs2 = [], []
    for u in src_subs:
        later = (sub < u) if rev else (sub > u)
        lhs2.append((qf * jnp.exp(jnp.where(later, b - e_sub[u], NEG))).astype(BF16))
        rhs2.append(jnp.where(sub == u, kt2, 0.0).astype(BF16))

    a3 = [jnp.zeros((c, c), F32) for _ in heads]
    for dist in range(GLA_SUB - 1, -1, -1):
        if dist == 0:
            prod = qf * k
        else:
            sh = (c - dist) if rev else dist
            ks = pltpu.roll(k, sh, 0)
            bs = pltpu.roll(b, sh, 0)
            prod = qf * ks * jnp.exp(jnp.minimum(b - bs, 0.0))
        for h in heads:
            rs = jnp.sum(_head(prod, h), axis=1, keepdims=True)
            a3[h] = jnp.where(cst["code"] == dist, rs, a3[h])

    e_end = b[0:1, :] if rev else b[c - 1:c, :]
    qd = (qf * jnp.exp(b)).astype(BF16)
    kdec = (k * jnp.exp(e_end - b)).astype(BF16)
    dec = jnp.exp(e_end)
    yield

    a = []
    for h in heads:
        a1 = _dot_nt(jnp.concatenate([_head(x, h) for x in lhs1], axis=1),
                     jnp.concatenate([_head(x, h) for x in rhs1], axis=1))
        a2 = _dot_nt(jnp.concatenate([_head(x, h) for x in lhs2], axis=1),
                     jnp.concatenate([_head(x, h) for x in rhs2], axis=1))
        a.append((a1 + jnp.where(cst["same_blk"], a2, 0.0) + a3[h]).astype(BF16))
    yield

    outs = []
    for h in heads:
        st = state[h]
        outs.append(_dot(a[h], _head(v, h)) + _dot_nt(_head(qd, h), st.astype(BF16)))
        state[h] = st * _head(dec, h) + _dot_tn(_head(v, h), _head(kdec, h))
    emit(jnp.concatenate(outs, axis=1))


def _lower_bound(lb_ref):
    p = lb_ref[...]
    e = jnp.exp(p - jnp.max(p, axis=0, keepdims=True))
    return e[0:1, :] / jnp.sum(e, axis=0, keepdims=True)


def _run_round_robin(gens):
    live = list(gens)
    while live:
        nxt = []
        for gen in live:
            try:
                next(gen)
                nxt.append(gen)
            except StopIteration:
                pass
        live = nxt


def _gla_kernel(qf_ref, vf_ref, ff_ref, qb_ref, vb_ref, fb_ref, lbf_ref, lbb_ref,
                of_ref, ob_ref, st_ref, *, tb, chunk):
    n_chunks = tb // chunk

    @pl.when(pl.program_id(1) == 0)
    def _():
        st_ref[...] = jnp.zeros(st_ref.shape, F32)

    dirs = (
        (False, qf_ref, vf_ref, ff_ref, of_ref, _lower_bound(lbf_ref), _gla_consts(chunk, False)),
        (True, qb_ref, vb_ref, fb_ref, ob_ref, _lower_bound(lbb_ref), _gla_consts(chunk, True)),
    )

    def body(ci, carry):
        gens, states = [], []
        for di, (rev, q_ref, v_ref, f_ref, o_ref, lb, cst) in enumerate(dirs):
            cidx = (n_chunks - 1 - ci) if rev else ci
            rows = pl.ds(pl.multiple_of(cidx * chunk, chunk), chunk)
            state = [st_ref[di, h] for h in range(HG_HEADS)]

            def emit(o, o_ref=o_ref, rows=rows):
                o_ref[0, rows, :] = o

            gens.append(_gla_chunk_steps(q_ref[0, rows, :], v_ref[0, rows, :], f_ref[0, rows, :],
                                         lb, state, emit, cst, rev))
            states.append(state)
        _run_round_robin(gens)
        for di, state in enumerate(states):
            for h in range(HG_HEADS):
                st_ref[di, h] = state[h]
        return carry

    lax.fori_loop(0, n_chunks, body, 0)


def _gla_call(qh, ih, ff, fb, lbf, lbb, tb, chunk):
    bsz, t, _ = qh.shape
    nt = t // tb
    fwd = pl.BlockSpec((1, tb, HG_WIDTH), lambda b, i: (b, i, 0))
    bwd = pl.BlockSpec((1, tb, HG_WIDTH), lambda b, i: (b, nt - 1 - i, 0))
    lbs = pl.BlockSpec(lbf.shape, lambda b, i: (0, 0))
    kern = functools.partial(_gla_kernel, tb=tb, chunk=chunk)
    return pl.pallas_call(
        kern,
        grid=(bsz, nt),
        in_specs=[fwd, fwd, fwd, bwd, bwd, bwd, lbs, lbs],
        out_specs=[fwd, bwd],
        out_shape=[jax.ShapeDtypeStruct((bsz, t, HG_WIDTH), F32)] * 2,
        scratch_shapes=[pltpu.VMEM((2, HG_HEADS, LANES, HG_DK), F32)],
        compiler_params=pltpu.CompilerParams(
            dimension_semantics=("arbitrary", "arbitrary"), vmem_limit_bytes=VMEM_LIMIT),
        name="gla",
    )(qh, ih, ff, qh, ih, fb, lbf, lbb)


def _post_kernel(x1_ref, oa_ref, of_ref, ob_ref, gh_ref, hg_ref, wout_ref,
                 ln2g_ref, ln2b_ref, w13_ref, w2_ref, ln3g_ref, ln3b_ref, out_ref):
    o = of_ref[...] + ob_ref[...]
    parts = []
    for h in range(HG_HEADS):
        oh = o[:, h * LANES:(h + 1) * LANES]
        ms = jnp.mean(oh * oh, axis=1, keepdims=True)
        parts.append(oh * lax.rsqrt(ms + EPS))
    gate = gh_ref[...].astype(F32)
    ohg = jnp.concatenate(parts, axis=1) * hg_ref[...] * (gate * jax.nn.sigmoid(gate))
    mix = (_dot(oa_ref[...], wout_ref[:ATTN_WIDTH, :])
           + _dot(ohg.astype(BF16), wout_ref[ATTN_WIDTH:, :]))
    x2 = _layer_norm(ALPHA * x1_ref[...] + mix, ln2g_ref[...], ln2b_ref[...])
    ffn = _swiglu(x2.astype(BF16), w13_ref, w2_ref)
    out_ref[...] = _layer_norm(ALPHA * x2 + 0.5 * ffn, ln3g_ref[...], ln3b_ref[...])


def _post_call(x1, oa, of, ob, gh, hg, wout, ln2g, ln2b, w13, w2, ln3g, ln3b, tm):
    m = x1.shape[0]
    return pl.pallas_call(
        _post_kernel,
        grid=(m // tm,),
        in_specs=[_rows(tm, D_MODEL), _rows(tm, ATTN_WIDTH), _rows(tm, HG_WIDTH),
                  _rows(tm, HG_WIDTH), _rows(tm, HG_WIDTH), _resident(hg.shape),
                  _resident(wout.shape), _resident(ln2g.shape), _resident(ln2b.shape),
                  _resident(w13.shape), _resident(w2.shape), _resident(ln3g.shape),
                  _resident(ln3b.shape)],
        out_specs=_rows(tm, D_MODEL),
        out_shape=jax.ShapeDtypeStruct((m, D_MODEL), F32),
        compiler_params=pltpu.CompilerParams(
            dimension_semantics=("arbitrary",), vmem_limit_bytes=VMEM_LIMIT),
        name="post",
    )(x1, oa, of, ob, gh, hg, wout, ln2g, ln2b, w13, w2, ln3g, ln3b)


def _prep_weights(ln_g, ln_b, ffn_w13, ffn_w2, w_in, attn_sink, attn_norm_g,
                  hg_lb_fwd, hg_lb_bwd, hg_norm_g, w_out):
    l = 0
    sizes = (ATTN_WIDTH, KV_WIDTH, KV_WIDTH, HG_WIDTH, HG_WIDTH, HG_WIDTH, HG_WIDTH, HG_WIDTH)
    offs = [0]
    for s in sizes:
        offs.append(offs[-1] + s)
    cols = [w_in[l][:, offs[i]:offs[i + 1]] for i in range(len(sizes))]
    q_a, k_a, v_a, q_h, f_f, f_b, i_h, g_h = cols
    win = jnp.concatenate([q_a * HEAD_DIM ** -0.5, k_a, v_a, q_h, i_h, g_h, f_f, f_b], axis=1)
    row = lambda a: a.reshape(1, -1).astype(F32)
    return dict(
        w13=[ffn_w13[l, i].astype(BF16) for i in range(2)],
        w2=[ffn_w2[l, i].astype(BF16) for i in range(2)],
        lng=[row(ln_g[l, i]) for i in range(3)],
        lnb=[row(ln_b[l, i]) for i in range(3)],
        win=win.astype(BF16),
        sink=attn_sink[l].astype(F32),
        ng=row(attn_norm_g[l]),
        lbf=hg_lb_fwd.astype(F32),
        lbb=hg_lb_bwd.astype(F32),
        hg=row(hg_norm_g[l]),
        wout=w_out[l].astype(BF16),
    )


def _largest_tile(n, cap, unit):
    t = min(cap, n)
    while n % t or t % unit:
        t -= unit
    return t


def _layer(x, w):
    bsz, t, d = x.shape
    m = bsz * t
    tm = _largest_tile(m, ROW_TILE, 8)
    tq = _largest_tile(t, ATTN_TILE, WINDOW)
    tb = _largest_tile(t, GLA_TILE, GLA_CHUNK)
    x1, qa, kv, qh, ih, gh, ff, fb = _pre_call(
        x.reshape(m, d), w["w13"][0], w["w2"][0], w["lng"][0], w["lnb"][0], w["win"], tm)
    seq = lambda a: a.reshape(bsz, t, a.shape[-1])
    oa = _attn_call(w["sink"], seq(qa), seq(kv), w["ng"], tq)
    of, ob = _gla_call(seq(qh), seq(ih), seq(ff), seq(fb), w["lbf"], w["lbb"], tb, GLA_CHUNK)
    flat = lambda a: a.reshape(m, a.shape[-1])
    y = _post_call(x1, flat(oa), flat(of), flat(ob), gh, w["hg"], w["wout"],
                   w["lng"][1], w["lnb"][1], w["w13"][1], w["w2"][1],
                   w["lng"][2], w["lnb"][2], tm)
    return y.reshape(bsz, t, d)


def kernel(x_prompt, x_sample, ln_g, ln_b, ffn_w13, ffn_w2, w_in, attn_sink, attn_norm_g,
           hg_lb_fwd, hg_lb_bwd, hg_norm_g, w_out):
    w = _prep_weights(ln_g, ln_b, ffn_w13, ffn_w2, w_in, attn_sink, attn_norm_g,
                      hg_lb_fwd, hg_lb_bwd, hg_norm_g, w_out)
    return (_layer(x_prompt, w), _layer(x_sample, w))
```

```python
import functools

import jax
import jax.numpy as jnp
from jax import lax
from jax.experimental import pallas as pl
from jax.experimental.pallas import tpu as pltpu

F32 = jnp.float32
BF16 = jnp.bfloat16

D_MODEL = 1024
DEPTH = 1
ATTN_WIDTH = 512
HEAD_DIM = 64
N_HEADS = 8
N_KV_HEADS = 2
KV_WIDTH = N_KV_HEADS * HEAD_DIM
WINDOW = 128
HG_WIDTH = 512
HG_HEADS = 4
HG_DK = 128
D_FF = 2816
ALPHA = (2.0 * DEPTH) ** 0.25
EPS = 1e-5
NEG = -1e30

LANES = 128
VMEM_LIMIT = 56 * 1024 * 1024

ROW_TILE = 256
ATTN_TILE = 512
GLA_TILE = 512
GLA_CHUNK = 128
GLA_BLK = 16
GLA_SUB = 4


def _dot(a, b):
    return jnp.dot(a, b, preferred_element_type=F32)


def _dot_nt(a, b):
    return lax.dot_general(a, b, (((1,), (1,)), ((), ())), preferred_element_type=F32)


def _dot_tn(a, b):
    return lax.dot_general(a, b, (((0,), (0,)), ((), ())), preferred_element_type=F32)


def _layer_norm(y, g, b):
    mu = jnp.mean(y, axis=-1, keepdims=True)
    yc = y - mu
    var = jnp.mean(yc * yc, axis=-1, keepdims=True)
    return yc * lax.rsqrt(var + EPS) * g + b


def _swiglu(xb, w13_ref, w2_ref):
    h = _dot(xb, w13_ref[...])
    gate = h[:, :D_FF]
    up = h[:, D_FF:]
    act = (gate * jax.nn.sigmoid(gate)) * up
    return _dot(act.astype(BF16), w2_ref[...])


def _pre_kernel(x_ref, w13_ref, w2_ref, lng_ref, lnb_ref, win_ref,
                x1_ref, qa_ref, kv_ref, qh_ref, ih_ref, gh_ref, ff_ref, fb_ref):
    x = x_ref[...]
    ffn = _swiglu(x.astype(BF16), w13_ref, w2_ref)
    x1 = _layer_norm(ALPHA * x + 0.5 * ffn, lng_ref[...], lnb_ref[...])
    x1_ref[...] = x1
    proj = _dot(x1.astype(BF16), win_ref[...])
    off = 0
    for ref in (qa_ref, kv_ref, qh_ref, ih_ref, gh_ref, ff_ref, fb_ref):
        w = ref.shape[-1]
        ref[...] = proj[:, off:off + w].astype(ref.dtype)
        off += w


def _resident(shape):
    nd = len(shape)
    return pl.BlockSpec(shape, lambda *_: (0,) * nd, pipeline_mode=pl.Buffered(1))


def _rows(tm, width):
    return pl.BlockSpec((tm, width), lambda i: (i, 0))


def _pre_call(x, w13, w2, lng, lnb, win, tm):
    m = x.shape[0]
    widths = (ATTN_WIDTH, 2 * KV_WIDTH, HG_WIDTH, HG_WIDTH, HG_WIDTH, HG_WIDTH, HG_WIDTH)
    dtypes = (BF16, BF16, BF16, BF16, BF16, F32, F32)
    out_shape = [jax.ShapeDtypeStruct((m, D_MODEL), F32)]
    out_shape += [jax.ShapeDtypeStruct((m, w), d) for w, d in zip(widths, dtypes)]
    out_specs = [_rows(tm, D_MODEL)] + [_rows(tm, w) for w in widths]
    return pl.pallas_call(
        _pre_kernel,
        grid=(m // tm,),
        in_specs=[_rows(tm, D_MODEL), _resident(w13.shape), _resident(w2.shape),
                  _resident(lng.shape), _resident(lnb.shape), _resident(win.shape)],
        out_specs=out_specs,
        out_shape=out_shape,
        compiler_params=pltpu.CompilerParams(
            dimension_semantics=("arbitrary",), vmem_limit_bytes=VMEM_LIMIT),
        name="pre",
    )(x, w13, w2, lng, lnb, win)


def _attn_kernel(sink_ref, q_ref, kvp_ref, kvc_ref, kvn_ref, ng_ref, o_ref,
                 kx_ref, vx_ref, bias_ref, *, tq):
    b_id = pl.program_id(0)
    i_id = pl.program_id(1)
    n_sub = tq // WINDOW
    span = 3 * WINDOW

    @pl.when((b_id == 0) & (i_id == 0))
    def _():
        r = lax.broadcasted_iota(jnp.int32, (WINDOW, span), 0)
        c = lax.broadcasted_iota(jnp.int32, (WINDOW, span), 1)
        dist = jnp.abs(c - WINDOW - r).astype(F32)
        for h in range(N_HEADS):
            slope = 2.0 ** (-8.0 * (h + 1) / N_HEADS)
            bias_ref[h] = jnp.where(dist <= float(WINDOW), -slope * dist, NEG)

    for lo_row, ref, n in ((0, kvp_ref, WINDOW), (WINDOW, kvc_ref, tq),
                           (WINDOW + tq, kvn_ref, WINDOW)):
        kv = ref[0]
        low = lax.broadcasted_iota(jnp.int32, (n, LANES), 1) < HEAD_DIM
        zero = jnp.zeros((n, LANES), BF16)
        for x, dst in ((kv[:, :LANES], kx_ref), (kv[:, LANES:], vx_ref)):
            xr = jnp.concatenate([x[:, HEAD_DIM:], x[:, :HEAD_DIM]], axis=1)
            rows = pl.ds(lo_row, n)
            dst[0, rows, :] = jnp.where(low, x, zero)
            dst[1, rows, :] = jnp.where(low, zero, xr)
            dst[2, rows, :] = jnp.where(low, xr, zero)
            dst[3, rows, :] = jnp.where(low, zero, x)

    col = lax.broadcasted_iota(jnp.int32, (WINDOW, span), 1)
    pen_lo = jnp.where(i_id == 0, NEG, 0.0).astype(F32)
    pen_hi = jnp.where(i_id == pl.num_programs(1) - 1, NEG, 0.0).astype(F32)
    low_out = lax.broadcasted_iota(jnp.int32, (WINDOW, LANES), 1) < HEAD_DIM
    gain = ng_ref[...]

    def block_steps(j):
        rows = slice(j * WINDOW, (j + 1) * WINDOW)
        srows = slice(j * WINDOW, j * WINDOW + span)
        logit = {}
        for g in range(N_KV_HEADS):
            qs = jnp.concatenate(
                [q_ref[0, rows, g * 256:g * 256 + LANES],
                 q_ref[0, rows, g * 256 + LANES:(g + 1) * 256]], axis=0)
            for e in range(2):
                logit[g, e] = _dot_nt(qs, kx_ref[2 * g + e, srows, :])
        yield

        probs, dens = {}, {}
        for g in range(N_KV_HEADS):
            for r in range(2):
                for e in range(2):
                    h = 4 * g + 2 * r + e
                    l = logit[g, e][r * WINDOW:(r + 1) * WINDOW, :] + bias_ref[h]
                    if j == 0:
                        l = l + jnp.where(col < WINDOW, pen_lo, 0.0)
                    if j == n_sub - 1:
                        l = l + jnp.where(col >= 2 * WINDOW, pen_hi, 0.0)
                    sk = sink_ref[h]
                    m = jnp.maximum(jnp.max(l, axis=1, keepdims=True), sk)
                    p = jnp.exp(l - m)
                    dens[h] = jnp.sum(p, axis=1, keepdims=True) + jnp.exp(sk - m)
                    probs[h] = p.astype(BF16)
        yield

        outs = []
        for g in range(N_KV_HEADS):
            for r in range(2):
                h = 4 * g + 2 * r
                acc = (_dot(probs[h], vx_ref[2 * g, srows, :])
                       + _dot(probs[h + 1], vx_ref[2 * g + 1, srows, :]))
                outs.append(acc * jnp.where(low_out, 1.0 / dens[h], 1.0 / dens[h + 1]))
        oj = jnp.concatenate(outs, axis=1)
        ms = jnp.mean(oj * oj, axis=1, keepdims=True)
        o_ref[0, rows, :] = (oj * lax.rsqrt(ms + EPS) * gain).astype(o_ref.dtype)

    _run_staggered([block_steps(j) for j in range(n_sub)])


def _attn_call(sink, qa, kv, ng, tq):
    bsz, t, _ = qa.shape
    r = tq // WINDOW
    nblk = t // WINDOW
    kern = functools.partial(_attn_kernel, tq=tq)
    return pl.pallas_call(
        kern,
        grid=(bsz, t // tq),
        in_specs=[
            pl.BlockSpec(memory_space=pltpu.SMEM),
            pl.BlockSpec((1, tq, ATTN_WIDTH), lambda b, i: (b, i, 0)),
            pl.BlockSpec((1, WINDOW, 2 * KV_WIDTH), lambda b, i: (b, jnp.maximum(i * r - 1, 0), 0)),
            pl.BlockSpec((1, tq, 2 * KV_WIDTH), lambda b, i: (b, i, 0)),
            pl.BlockSpec((1, WINDOW, 2 * KV_WIDTH),
                         lambda b, i: (b, jnp.minimum((i + 1) * r, nblk - 1), 0)),
            pl.BlockSpec((1, ATTN_WIDTH), lambda b, i: (0, 0)),
        ],
        out_specs=pl.BlockSpec((1, tq, ATTN_WIDTH), lambda b, i: (b, i, 0)),
        out_shape=jax.ShapeDtypeStruct((bsz, t, ATTN_WIDTH), BF16),
        scratch_shapes=[
            pltpu.VMEM((4, tq + 2 * WINDOW, LANES), BF16),
            pltpu.VMEM((4, tq + 2 * WINDOW, LANES), BF16),
            pltpu.VMEM((N_HEADS, WINDOW, 3 * WINDOW), F32),
        ],
        compiler_params=pltpu.CompilerParams(
            dimension_semantics=("arbitrary", "arbitrary"), vmem_limit_bytes=VMEM_LIMIT),
        name="attn",
    )(sink, qa, kv, kv, kv, ng)


def _gla_consts(c, rev):
    ti = lax.broadcasted_iota(jnp.int32, (c, c), 0)
    si = lax.broadcasted_iota(jnp.int32, (c, c), 1)
    d = (si - ti) if rev else (ti - si)
    cum = jnp.where(d >= 0, 1.0, 0.0).astype(BF16)
    same_blk = (ti // GLA_BLK) == (si // GLA_BLK)
    same_sub = (ti // GLA_SUB) == (si // GLA_SUB)
    code = jnp.where(same_sub, d, -1)
    row = lax.broadcasted_iota(jnp.int32, (c, HG_WIDTH), 0)
    return dict(cum=cum, same_blk=same_blk, code=code,
                sub=(row // GLA_SUB) % (GLA_BLK // GLA_SUB))


def _bcast_rows(b, rows, n):
    return jnp.concatenate(
        [jnp.broadcast_to(b[r:r + 1, :], (n, b.shape[1])) for r in rows], axis=0)


def _stack_rows(parts):
    return jnp.concatenate([p for p in parts if p.shape[0]], axis=0)


def _head(x, h):
    return x[:, h * LANES:(h + 1) * LANES]


def _pair(x, p):
    return x[:, 2 * p * LANES:2 * (p + 1) * LANES]


def _block_diag(x0, x1):
    top = jnp.concatenate([x0, jnp.zeros((x0.shape[0], x1.shape[1]), x0.dtype)], axis=1)
    bot = jnp.concatenate([jnp.zeros((x1.shape[0], x0.shape[1]), x1.dtype), x1], axis=1)
    return jnp.concatenate([top, bot], axis=0)


def _gla_chunk_steps(q_ref, v_ref, f_ref, lb, state, emit, cst, rev):
    q, v, flog = q_ref[...], v_ref[...], f_ref[...]
    c, width = flog.shape
    nb = c // GLA_BLK
    ns = GLA_BLK // GLA_SUB
    pairs = range(width // (2 * LANES))
    qf = q.astype(F32)
    s = jax.nn.sigmoid(flog)
    g = jnp.log2(lb + (1.0 - lb) * s)
    k = (1.0 - lb) * (1.0 - s)
    kb = k.astype(BF16)
    g1 = g.astype(BF16)
    g2 = (g - g1.astype(F32)).astype(BF16)
    gsplit = jnp.concatenate([g1, g2], axis=1)

    win = g
    near = [q]
    for dist in range(1, GLA_SUB):
        near.append((qf * jnp.exp2(win)).astype(BF16))
        if dist < GLA_SUB - 1:
            win = win + pltpu.roll(g, (c - dist) if rev else dist, 0)
    near = jnp.concatenate(near, axis=0)
    yield

    b2 = _dot(cst["cum"], gsplit)
    b = b2[:, :width] + b2[:, width:]
    p3 = [_dot_nt(_pair(near, p), _block_diag(_head(kb, 2 * p), _head(kb, 2 * p + 1)))
          for p in pairs]
    sub = cst["sub"]
    blk_end = [GLA_BLK * j + (0 if rev else GLA_BLK - 1) for j in range(nb)]
    sub_end = [[GLA_BLK * j + GLA_SUB * u + (0 if rev else GLA_SUB - 1) for j in range(nb)]
               for u in range(ns)]

    src_blocks = range(1, nb) if rev else range(nb - 1)
    e_blk = _bcast_rows(b, blk_end, GLA_BLK)
    kt = (k * jnp.exp2(e_blk - b)).astype(BF16)
    lhs1, rhs1 = [], []
    for j in src_blocks:
        lo, hi = GLA_BLK * j, GLA_BLK * (j + 1)
        later = slice(0, lo) if rev else slice(hi, c)
        e = b[blk_end[j]:blk_end[j] + 1, :]
        part = (qf[later] * jnp.exp2(b[later] - e)).astype(BF16)
        pad = jnp.zeros((c - part.shape[0], width), BF16)
        lhs1.append(jnp.concatenate([part, pad] if rev else [pad, part], axis=0))
        rhs1.append(_stack_rows([jnp.zeros((lo, width), BF16), kt[lo:hi],
                                 jnp.zeros((c - hi, width), BF16)]))

    e_sub = [_bcast_rows(b, sub_end[u], GLA_BLK) for u in range(ns)]
    e_own = e_sub[ns - 1]
    for u in range(ns - 2, -1, -1):
        e_own = jnp.where(sub == u, e_sub[u], e_own)
    kt2 = k * jnp.exp2(e_own - b)
    src_subs = range(1, ns) if rev else range(ns - 1)
    lhs2, rhs2 = [], []
    for u in src_subs:
        later = (sub < u) if rev else (sub > u)
        lhs2.append((qf * jnp.exp2(jnp.where(later, b - e_sub[u], NEG))).astype(BF16))
        rhs2.append(jnp.where(sub == u, kt2, 0.0).astype(BF16))

    e_end = b[0:1, :] if rev else b[c - 1:c, :]
    qd = (qf * jnp.exp2(b)).astype(BF16)
    kdec = (k * jnp.exp2(e_end - b)).astype(BF16)
    dec = jnp.exp2(e_end)
    yield

    a = []
    for p in pairs:
        h0, h1 = 2 * p, 2 * p + 1
        cat = lambda xs, h: jnp.concatenate([_head(x, h) for x in xs], axis=1)
        a1 = _dot_nt(jnp.concatenate([cat(lhs1, h0), cat(lhs1, h1)], axis=1),
                     _block_diag(cat(rhs1, h0), cat(rhs1, h1)))
        a2 = _dot_nt(jnp.concatenate([cat(lhs2, h0), cat(lhs2, h1)], axis=1),
                     _block_diag(cat(rhs2, h0), cat(rhs2, h1)))
        for e in range(2):
            cols = slice(e * c, (e + 1) * c)
            ah = a1[:, cols] + jnp.where(cst["same_blk"], a2[:, cols], 0.0)
            for dist in range(GLA_SUB):
                ah = jnp.where(cst["code"] == dist, p3[p][dist * c:(dist + 1) * c, cols], ah)
            a.append(ah.astype(BF16))
    yield

    outs = []
    for p in pairs:
        h0, h1 = 2 * p, 2 * p + 1
        st0, st1 = state[h0], state[h1]
        intra = _dot(jnp.concatenate([a[h0], a[h1]], axis=1),
                     _block_diag(_head(v, h0), _head(v, h1)))
        inter = _dot_nt(_pair(qd, p), _block_diag(st0.astype(BF16), st1.astype(BF16)))
        outs.append(intra + inter)
        state[h0] = st0 * _head(dec, h0) + _dot_tn(_head(v, h0), _head(kdec, h0))
        state[h1] = st1 * _head(dec, h1) + _dot_tn(_head(v, h1), _head(kdec, h1))
    emit(jnp.concatenate(outs, axis=1))


def _lower_bound(lb_ref):
    p = lb_ref[...]
    e = jnp.exp(p - jnp.max(p, axis=0, keepdims=True))
    return e[0:1, :] / jnp.sum(e, axis=0, keepdims=True)


def _run_staggered(gens):
    done = [False] * len(gens)
    rounds = 0
    while not all(done):
        for i, gen in enumerate(gens):
            if i <= rounds and not done[i]:
                try:
                    next(gen)
                except StopIteration:
                    done[i] = True
        rounds += 1


def _gla_kernel(qf_ref, vf_ref, ff_ref, qb_ref, vb_ref, fb_ref, lbf_ref, lbb_ref,
                of_ref, ob_ref, st_ref, *, tb, chunk):
    n_chunks = tb // chunk

    @pl.when(pl.program_id(1) == 0)
    def _():
        st_ref[...] = jnp.zeros(st_ref.shape, F32)

    dirs = (
        (False, qf_ref, vf_ref, ff_ref, of_ref, _lower_bound(lbf_ref), _gla_consts(chunk, False)),
        (True, qb_ref, vb_ref, fb_ref, ob_ref, _lower_bound(lbb_ref), _gla_consts(chunk, True)),
    )

    states = [[st_ref[di, h] for h in range(HG_HEADS)] for di in range(len(dirs))]
    gens = []
    for ci in range(n_chunks):
        for di, (rev, q_ref, v_ref, f_ref, o_ref, lb, cst) in enumerate(dirs):
            cidx = (n_chunks - 1 - ci) if rev else ci
            rows = slice(cidx * chunk, (cidx + 1) * chunk)

            def emit(o, o_ref=o_ref, rows=rows):
                o_ref[0, rows, :] = o

            gens.append(_gla_chunk_steps(q_ref.at[0, rows, :], v_ref.at[0, rows, :],
                                         f_ref.at[0, rows, :], lb, states[di], emit, cst, rev))
    _run_staggered(gens)
    for di, state in enumerate(states):
        for h in range(HG_HEADS):
            st_ref[di, h] = state[h]


def _gla_call(qh, ih, ff, fb, lbf, lbb, tb, chunk):
    bsz, t, _ = qh.shape
    nt = t // tb
    fwd = pl.BlockSpec((1, tb, HG_WIDTH), lambda b, i: (b, i, 0))
    bwd = pl.BlockSpec((1, tb, HG_WIDTH), lambda b, i: (b, nt - 1 - i, 0))
    lbs = pl.BlockSpec(lbf.shape, lambda b, i: (0, 0))
    kern = functools.partial(_gla_kernel, tb=tb, chunk=chunk)
    return pl.pallas_call(
        kern,
        grid=(bsz, nt),
        in_specs=[fwd, fwd, fwd, bwd, bwd, bwd, lbs, lbs],
        out_specs=[fwd, bwd],
        out_shape=[jax.ShapeDtypeStruct((bsz, t, HG_WIDTH), F32)] * 2,
        scratch_shapes=[pltpu.VMEM((2, HG_HEADS, LANES, HG_DK), F32)],
        compiler_params=pltpu.CompilerParams(
            dimension_semantics=("arbitrary", "arbitrary"), vmem_limit_bytes=VMEM_LIMIT),
        name="gla",
    )(qh, ih, ff, qh, ih, fb, lbf, lbb)


def _post_kernel(x1_ref, oa_ref, of_ref, ob_ref, gh_ref, hg_ref, wout_ref,
                 ln2g_ref, ln2b_ref, w13_ref, w2_ref, ln3g_ref, ln3b_ref, out_ref):
    o = of_ref[...] + ob_ref[...]
    parts = []
    for h in range(HG_HEADS):
        oh = o[:, h * LANES:(h + 1) * LANES]
        ms = jnp.mean(oh * oh, axis=1, keepdims=True)
        parts.append(oh * lax.rsqrt(ms + EPS))
    gate = gh_ref[...].astype(F32)
    ohg = jnp.concatenate(parts, axis=1) * hg_ref[...] * (gate * jax.nn.sigmoid(gate))
    mix = (_dot(oa_ref[...], wout_ref[:ATTN_WIDTH, :])
           + _dot(ohg.astype(BF16), wout_ref[ATTN_WIDTH:, :]))
    x2 = _layer_norm(ALPHA * x1_ref[...] + mix, ln2g_ref[...], ln2b_ref[...])
    ffn = _swiglu(x2.astype(BF16), w13_ref, w2_ref)
    out_ref[...] = _layer_norm(ALPHA * x2 + 0.5 * ffn, ln3g_ref[...], ln3b_ref[...])


def _post_call(x1, oa, of, ob, gh, hg, wout, ln2g, ln2b, w13, w2, ln3g, ln3b, tm):
    m = x1.shape[0]
    return pl.pallas_call(
        _post_kernel,
        grid=(m // tm,),
        in_specs=[_rows(tm, D_MODEL), _rows(tm, ATTN_WIDTH), _rows(tm, HG_WIDTH),
                  _rows(tm, HG_WIDTH), _rows(tm, HG_WIDTH), _resident(hg.shape),
                  _resident(wout.shape), _resident(ln2g.shape), _resident(ln2b.shape),
                  _resident(w13.shape), _resident(w2.shape), _resident(ln3g.shape),
                  _resident(ln3b.shape)],
        out_specs=_rows(tm, D_MODEL),
        out_shape=jax.ShapeDtypeStruct((m, D_MODEL), F32),
        compiler_params=pltpu.CompilerParams(
            dimension_semantics=("arbitrary",), vmem_limit_bytes=VMEM_LIMIT),
        name="post",
    )(x1, oa, of, ob, gh, hg, wout, ln2g, ln2b, w13, w2, ln3g, ln3b)


def _prep_weights(ln_g, ln_b, ffn_w13, ffn_w2, w_in, attn_sink, attn_norm_g,
                  hg_lb_fwd, hg_lb_bwd, hg_norm_g, w_out):
    l = 0
    sizes = (ATTN_WIDTH, KV_WIDTH, KV_WIDTH, HG_WIDTH, HG_WIDTH, HG_WIDTH, HG_WIDTH, HG_WIDTH)
    offs = [0]
    for s in sizes:
        offs.append(offs[-1] + s)
    cols = [w_in[l][:, offs[i]:offs[i + 1]] for i in range(len(sizes))]
    q_a, k_a, v_a, q_h, f_f, f_b, i_h, g_h = cols
    win = jnp.concatenate([q_a * HEAD_DIM ** -0.5, k_a, v_a, q_h, i_h, g_h, f_f, f_b], axis=1)
    row = lambda a: a.reshape(1, -1).astype(F32)
    return dict(
        w13=[ffn_w13[l, i].astype(BF16) for i in range(2)],
        w2=[ffn_w2[l, i].astype(BF16) for i in range(2)],
        lng=[row(ln_g[l, i]) for i in range(3)],
        lnb=[row(ln_b[l, i]) for i in range(3)],
        win=win.astype(BF16),
        sink=attn_sink[l].astype(F32),
        ng=row(attn_norm_g[l]),
        lbf=hg_lb_fwd.astype(F32),
        lbb=hg_lb_bwd.astype(F32),
        hg=row(hg_norm_g[l]),
        wout=w_out[l].astype(BF16),
    )


def _largest_tile(n, cap, unit):
    t = min(cap, n)
    while n % t or t % unit:
        t -= unit
    return t


def _layer(x, w):
    bsz, t, d = x.shape
    m = bsz * t
    tm = _largest_tile(m, ROW_TILE, 8)
    tq = _largest_tile(t, ATTN_TILE, WINDOW)
    tb = _largest_tile(t, GLA_TILE, GLA_CHUNK)
    x1, qa, kv, qh, ih, gh, ff, fb = _pre_call(
        x.reshape(m, d), w["w13"][0], w["w2"][0], w["lng"][0], w["lnb"][0], w["win"], tm)
    seq = lambda a: a.reshape(bsz, t, a.shape[-1])
    oa = _attn_call(w["sink"], seq(qa), seq(kv), w["ng"], tq)
    of, ob = _gla_call(seq(qh), seq(ih), seq(ff), seq(fb), w["lbf"], w["lbb"], tb, GLA_CHUNK)
    flat = lambda a: a.reshape(m, a.shape[-1])
    y = _post_call(x1, flat(oa), flat(of), flat(ob), gh, w["hg"], w["wout"],
                   w["lng"][1], w["lnb"][1], w["w13"][1], w["w2"][1],
                   w["lng"][2], w["lnb"][2], tm)
    return y.reshape(bsz, t, d)


def kernel(x_prompt, x_sample, ln_g, ln_b, ffn_w13, ffn_w2, w_in, attn_sink, attn_norm_g,
           hg_lb_fwd, hg_lb_bwd, hg_norm_g, w_out):
    w = _prep_weights(ln_g, ln_b, ffn_w13, ffn_w2, w_in, attn_sink, attn_norm_g,
                      hg_lb_fwd, hg_lb_bwd, hg_norm_g, w_out)
    return (_layer(x_prompt, w), _layer(x_sample, w))
```

```python
import functools

import jax
import jax.numpy as jnp
from jax import lax
from jax.experimental import pallas as pl
from jax.experimental.pallas import tpu as pltpu

F32 = jnp.float32
BF16 = jnp.bfloat16

D_MODEL = 1024
DEPTH = 1
ATTN_WIDTH = 512
HEAD_DIM = 64
N_HEADS = 8
N_KV_HEADS = 2
KV_WIDTH = N_KV_HEADS * HEAD_DIM
WINDOW = 128
HG_WIDTH = 512
HG_HEADS = 4
HG_DK = 128
D_FF = 2816
ALPHA = (2.0 * DEPTH) ** 0.25
EPS = 1e-5
NEG = -1e30

LANES = 128
VMEM_LIMIT = 56 * 1024 * 1024

ROW_TILE = 256
ATTN_TILE = 512
FFN_CHUNK = 256
GLA_CHUNK = 128
GLA_BLK = 16
GLA_SUB = 4


def _dot(a, b):
    return jnp.dot(a, b, preferred_element_type=F32)


def _dot_nt(a, b):
    return lax.dot_general(a, b, (((1,), (1,)), ((), ())), preferred_element_type=F32)


def _dot_tn(a, b):
    return lax.dot_general(a, b, (((0,), (0,)), ((), ())), preferred_element_type=F32)


def _layer_norm(y, g, b):
    mu = jnp.mean(y, axis=-1, keepdims=True)
    yc = y - mu
    var = jnp.mean(yc * yc, axis=-1, keepdims=True)
    return yc * lax.rsqrt(var + EPS) * g + b


def _ffn_steps(xb_ref, w13_ref, w2_ref, acc_ref):
    fc = FFN_CHUNK
    n = D_FF // fc
    hidden = {}

    def step(kk):
        def run():
            if kk < n:
                hidden[kk] = _dot(xb_ref[...], w13_ref[:, 2 * fc * kk:2 * fc * (kk + 1)])
            if kk > 0:
                h = hidden.pop(kk - 1)
                gate = h[:, :fc]
                act = (gate * jax.nn.sigmoid(gate)) * h[:, fc:]
                part = _dot(act.astype(BF16), w2_ref[fc * (kk - 1):fc * kk, :])
                if kk == 1:
                    acc_ref[...] = part
                else:
                    acc_ref[...] += part
        return run

    return [step(kk) for kk in range(n + 1)]


def _interleave(major, minor):
    out, j = [], 0
    for i, step in enumerate(major):
        out.append(step)
        while j < len(minor) and (j + 1) * len(major) <= (i + 1) * len(minor):
            out.append(minor[j])
            j += 1
    return out + minor[j:]


def _resident(shape):
    nd = len(shape)
    return pl.BlockSpec(shape, lambda *_: (0,) * nd, pipeline_mode=pl.Buffered(1))


def _attn_kernel(sink_ref, q_ref, kvp_ref, kvc_ref, kvn_ref, ng_ref, o_ref,
                 kx_ref, vx_ref, bias_ref, *, tq):
    b_id = pl.program_id(0)
    i_id = pl.program_id(1)
    n_sub = tq // WINDOW
    span = 3 * WINDOW

    @pl.when((b_id == 0) & (i_id == 0))
    def _():
        r = lax.broadcasted_iota(jnp.int32, (WINDOW, span), 0)
        c = lax.broadcasted_iota(jnp.int32, (WINDOW, span), 1)
        dist = jnp.abs(c - WINDOW - r).astype(F32)
        for h in range(N_HEADS):
            slope = 2.0 ** (-8.0 * (h + 1) / N_HEADS)
            bias_ref[h] = jnp.where(dist <= float(WINDOW), -slope * dist, NEG)

    for lo_row, ref, n in ((0, kvp_ref, WINDOW), (WINDOW, kvc_ref, tq),
                           (WINDOW + tq, kvn_ref, WINDOW)):
        kv = ref[0]
        low = lax.broadcasted_iota(jnp.int32, (n, LANES), 1) < HEAD_DIM
        zero = jnp.zeros((n, LANES), BF16)
        for x, dst in ((kv[:, :LANES], kx_ref), (kv[:, LANES:], vx_ref)):
            xr = jnp.concatenate([x[:, HEAD_DIM:], x[:, :HEAD_DIM]], axis=1)
            rows = pl.ds(lo_row, n)
            dst[0, rows, :] = jnp.where(low, x, zero)
            dst[1, rows, :] = jnp.where(low, zero, xr)
            dst[2, rows, :] = jnp.where(low, xr, zero)
            dst[3, rows, :] = jnp.where(low, zero, x)

    col = lax.broadcasted_iota(jnp.int32, (WINDOW, span), 1)
    pen_lo = jnp.where(i_id == 0, NEG, 0.0).astype(F32)
    pen_hi = jnp.where(i_id == pl.num_programs(1) - 1, NEG, 0.0).astype(F32)
    low_out = lax.broadcasted_iota(jnp.int32, (WINDOW, LANES), 1) < HEAD_DIM
    gain = ng_ref[...]

    def block_steps(j):
        rows = slice(j * WINDOW, (j + 1) * WINDOW)
        srows = slice(j * WINDOW, j * WINDOW + span)
        logit = {}
        for g in range(N_KV_HEADS):
            qs = jnp.concatenate(
                [q_ref[0, rows, g * 256:g * 256 + LANES],
                 q_ref[0, rows, g * 256 + LANES:(g + 1) * 256]], axis=0)
            for e in range(2):
                logit[g, e] = _dot_nt(qs, kx_ref[2 * g + e, srows, :])
        yield

        probs, dens = {}, {}
        for g in range(N_KV_HEADS):
            for r in range(2):
                for e in range(2):
                    h = 4 * g + 2 * r + e
                    l = logit[g, e][r * WINDOW:(r + 1) * WINDOW, :] + bias_ref[h]
                    if j == 0:
                        l = l + jnp.where(col < WINDOW, pen_lo, 0.0)
                    if j == n_sub - 1:
                        l = l + jnp.where(col >= 2 * WINDOW, pen_hi, 0.0)
                    sk = sink_ref[h]
                    m = jnp.maximum(jnp.max(l, axis=1, keepdims=True), sk)
                    p = jnp.exp(l - m)
                    dens[h] = jnp.sum(p, axis=1, keepdims=True) + jnp.exp(sk - m)
                    probs[h] = p.astype(BF16)
        yield

        outs = []
        for g in range(N_KV_HEADS):
            for r in range(2):
                h = 4 * g + 2 * r
                acc = (_dot(probs[h], vx_ref[2 * g, srows, :])
                       + _dot(probs[h + 1], vx_ref[2 * g + 1, srows, :]))
                outs.append(acc * jnp.where(low_out, 1.0 / dens[h], 1.0 / dens[h + 1]))
        oj = jnp.concatenate(outs, axis=1)
        ms = jnp.mean(oj * oj, axis=1, keepdims=True)
        o_ref[0, rows, :] = (oj * lax.rsqrt(ms + EPS) * gain).astype(o_ref.dtype)

    _run_staggered([block_steps(j) for j in range(n_sub)])


def _attn_call(sink, qa, kv, ng, tq):
    bsz, t, _ = qa.shape
    r = tq // WINDOW
    nblk = t // WINDOW
    kern = functools.partial(_attn_kernel, tq=tq)
    return pl.pallas_call(
        kern,
        grid=(bsz, t // tq),
        in_specs=[
            pl.BlockSpec(memory_space=pltpu.SMEM),
            pl.BlockSpec((1, tq, ATTN_WIDTH), lambda b, i: (b, i, 0)),
            pl.BlockSpec((1, WINDOW, 2 * KV_WIDTH), lambda b, i: (b, jnp.maximum(i * r - 1, 0), 0)),
            pl.BlockSpec((1, tq, 2 * KV_WIDTH), lambda b, i: (b, i, 0)),
            pl.BlockSpec((1, WINDOW, 2 * KV_WIDTH),
                         lambda b, i: (b, jnp.minimum((i + 1) * r, nblk - 1), 0)),
            pl.BlockSpec((1, ATTN_WIDTH), lambda b, i: (0, 0)),
        ],
        out_specs=pl.BlockSpec((1, tq, ATTN_WIDTH), lambda b, i: (b, i, 0)),
        out_shape=jax.ShapeDtypeStruct((bsz, t, ATTN_WIDTH), BF16),
        scratch_shapes=[
            pltpu.VMEM((4, tq + 2 * WINDOW, LANES), BF16),
            pltpu.VMEM((4, tq + 2 * WINDOW, LANES), BF16),
            pltpu.VMEM((N_HEADS, WINDOW, 3 * WINDOW), F32),
        ],
        compiler_params=pltpu.CompilerParams(
            dimension_semantics=("arbitrary", "arbitrary"), vmem_limit_bytes=VMEM_LIMIT),
        name="attn",
    )(sink, qa, kv, kv, kv, ng)


def _gla_consts(c, rev):
    ti = lax.broadcasted_iota(jnp.int32, (c, c), 0)
    si = lax.broadcasted_iota(jnp.int32, (c, c), 1)
    d = (si - ti) if rev else (ti - si)
    cum = jnp.where(d >= 0, 1.0, 0.0).astype(BF16)
    same_blk = (ti // GLA_BLK) == (si // GLA_BLK)
    same_sub = (ti // GLA_SUB) == (si // GLA_SUB)
    code = jnp.where(same_sub, d, -1)
    row = lax.broadcasted_iota(jnp.int32, (c, HG_WIDTH), 0)
    return dict(cum=cum, same_blk=same_blk, code=code,
                sub=(row // GLA_SUB) % (GLA_BLK // GLA_SUB))


def _bcast_rows(b, rows, n):
    return jnp.concatenate(
        [jnp.broadcast_to(b[r:r + 1, :], (n, b.shape[1])) for r in rows], axis=0)


def _stack_rows(parts):
    return jnp.concatenate([p for p in parts if p.shape[0]], axis=0)


def _head(x, h):
    return x[:, h * LANES:(h + 1) * LANES]


def _pair(x, p):
    return x[:, 2 * p * LANES:2 * (p + 1) * LANES]


def _block_diag(x0, x1):
    top = jnp.concatenate([x0, jnp.zeros((x0.shape[0], x1.shape[1]), x0.dtype)], axis=1)
    bot = jnp.concatenate([jnp.zeros((x1.shape[0], x0.shape[1]), x1.dtype), x1], axis=1)
    return jnp.concatenate([top, bot], axis=0)


def _gla_chunk_steps(q_ref, v_ref, f_ref, lb, state, emit, cst, rev):
    q, v, flog = q_ref[...], v_ref[...], f_ref[...]
    c, width = flog.shape
    nb = c // GLA_BLK
    ns = GLA_BLK // GLA_SUB
    pairs = range(width // (2 * LANES))
    qf = q.astype(F32)
    s = jax.nn.sigmoid(flog)
    g = jnp.log2(lb + (1.0 - lb) * s)
    k = (1.0 - lb) * (1.0 - s)
    kb = k.astype(BF16)
    g1 = g.astype(BF16)
    g2 = (g - g1.astype(F32)).astype(BF16)
    gsplit = jnp.concatenate([g1, g2], axis=1)

    win = g
    near = [q]
    for dist in range(1, GLA_SUB):
        near.append((qf * jnp.exp2(win)).astype(BF16))
        if dist < GLA_SUB - 1:
            win = win + pltpu.roll(g, (c - dist) if rev else dist, 0)
    near = jnp.concatenate(near, axis=0)
    yield

    b2 = _dot(cst["cum"], gsplit)
    b = b2[:, :width] + b2[:, width:]
    p3 = [_dot_nt(_pair(near, p), _block_diag(_head(kb, 2 * p), _head(kb, 2 * p + 1)))
          for p in pairs]
    sub = cst["sub"]
    blk_end = [GLA_BLK * j + (0 if rev else GLA_BLK - 1) for j in range(nb)]
    sub_end = [[GLA_BLK * j + GLA_SUB * u + (0 if rev else GLA_SUB - 1) for j in range(nb)]
               for u in range(ns)]

    src_blocks = range(1, nb) if rev else range(nb - 1)
    e_blk = _bcast_rows(b, blk_end, GLA_BLK)
    kt = (k * jnp.exp2(e_blk - b)).astype(BF16)
    lhs1, rhs1 = [], []
    for j in src_blocks:
        lo, hi = GLA_BLK * j, GLA_BLK * (j + 1)
        later = slice(0, lo) if rev else slice(hi, c)
        e = b[blk_end[j]:blk_end[j] + 1, :]
        part = (qf[later] * jnp.exp2(b[later] - e)).astype(BF16)
        pad = jnp.zeros((c - part.shape[0], width), BF16)
        lhs1.append(jnp.concatenate([part, pad] if rev else [pad, part], axis=0))
        rhs1.append(_stack_rows([jnp.zeros((lo, width), BF16), kt[lo:hi],
                                 jnp.zeros((c - hi, width), BF16)]))

    e_sub = [_bcast_rows(b, sub_end[u], GLA_BLK) for u in range(ns)]
    e_own = e_sub[ns - 1]
    for u in range(ns - 2, -1, -1):
        e_own = jnp.where(sub == u, e_sub[u], e_own)
    kt2 = k * jnp.exp2(e_own - b)
    src_subs = range(1, ns) if rev else range(ns - 1)
    lhs2, rhs2 = [], []
    for u in src_subs:
        later = (sub < u) if rev else (sub > u)
        lhs2.append((qf * jnp.exp2(jnp.where(later, b - e_sub[u], NEG))).astype(BF16))
        rhs2.append(jnp.where(sub == u, kt2, 0.0).astype(BF16))

    e_end = b[0:1, :] if rev else b[c - 1:c, :]
    qd = (qf * jnp.exp2(b)).astype(BF16)
    kdec = (k * jnp.exp2(e_end - b)).astype(BF16)
    dec = jnp.exp2(e_end)
    yield

    a = []
    for p in pairs:
        h0, h1 = 2 * p, 2 * p + 1
        cat = lambda xs, h: jnp.concatenate([_head(x, h) for x in xs], axis=1)
        a1 = _dot_nt(jnp.concatenate([cat(lhs1, h0), cat(lhs1, h1)], axis=1),
                     _block_diag(cat(rhs1, h0), cat(rhs1, h1)))
        a2 = _dot_nt(jnp.concatenate([cat(lhs2, h0), cat(lhs2, h1)], axis=1),
                     _block_diag(cat(rhs2, h0), cat(rhs2, h1)))
        for e in range(2):
            cols = slice(e * c, (e + 1) * c)
            ah = a1[:, cols] + jnp.where(cst["same_blk"], a2[:, cols], 0.0)
            for dist in range(GLA_SUB):
                ah = jnp.where(cst["code"] == dist, p3[p][dist * c:(dist + 1) * c, cols], ah)
            a.append(ah.astype(BF16))
    yield

    outs = []
    for p in pairs:
        h0, h1 = 2 * p, 2 * p + 1
        st0, st1 = state[h0], state[h1]
        intra = _dot(jnp.concatenate([a[h0], a[h1]], axis=1),
                     _block_diag(_head(v, h0), _head(v, h1)))
        inter = _dot_nt(_pair(qd, p), _block_diag(st0.astype(BF16), st1.astype(BF16)))
        outs.append(intra + inter)
        state[h0] = st0 * _head(dec, h0) + _dot_tn(_head(v, h0), _head(kdec, h0))
        state[h1] = st1 * _head(dec, h1) + _dot_tn(_head(v, h1), _head(kdec, h1))
    emit(jnp.concatenate(outs, axis=1))


def _lower_bound(lb_ref):
    p = lb_ref[...]
    e = jnp.exp(p - jnp.max(p, axis=0, keepdims=True))
    return e[0:1, :] / jnp.sum(e, axis=0, keepdims=True)


def _run_staggered(gens):
    done = [False] * len(gens)
    rounds = 0
    while not all(done):
        for i, gen in enumerate(gens):
            if i <= rounds and not done[i]:
                try:
                    next(gen)
                except StopIteration:
                    done[i] = True
        rounds += 1


def _scan_steps(q_ref, v_ref, f_ref, lb, state, o_ref, rev):
    n = q_ref.shape[0] // GLA_CHUNK
    cst = _gla_consts(GLA_CHUNK, rev)
    gens = []
    for ci in (range(n - 1, -1, -1) if rev else range(n)):
        rows = slice(ci * GLA_CHUNK, (ci + 1) * GLA_CHUNK)

        def emit(o, rows=rows):
            o_ref[rows, :] = o

        gens.append(_gla_chunk_steps(q_ref.at[rows, :], v_ref.at[rows, :], f_ref.at[rows, :],
                                     lb, state, emit, cst, rev))
    steps = []

    def advance(gen):
        def run():
            next(gen, None)
        return run

    depth = 4
    for rnd in range(len(gens) + depth - 1):
        for i, gen in enumerate(gens):
            if 0 <= rnd - i < depth:
                steps.append(advance(gen))
    return steps


def _fwd_kernel(x_ref, w13_ref, w2_ref, lng_ref, lnb_ref, win_ref, lbf_ref,
                x1_ref, qa_ref, kv_ref, qh_ref, ih_ref, gh_ref, fb_ref, of_ref,
                qbuf_ref, ibuf_ref, fbuf_ref, st_ref, xb_ref, acc_ref,
                *, tiles_per_seq):
    s = pl.program_id(0)
    cur = lax.rem(s, 2)
    prv = 1 - cur

    @pl.when(s == 0)
    def _():
        qbuf_ref[...] = jnp.zeros(qbuf_ref.shape, BF16)
        ibuf_ref[...] = jnp.zeros(ibuf_ref.shape, BF16)
        fbuf_ref[...] = jnp.zeros(fbuf_ref.shape, F32)
        st_ref[...] = jnp.zeros(st_ref.shape, F32)

    restart = lax.rem(s - 1, tiles_per_seq) == 0
    state = [jnp.where(restart, 0.0, st_ref[h]) for h in range(HG_HEADS)]
    scan = _scan_steps(qbuf_ref.at[prv], ibuf_ref.at[prv], fbuf_ref.at[prv],
                       _lower_bound(lbf_ref), state, of_ref, False)

    xb_ref[...] = x_ref[...].astype(BF16)
    pre = _ffn_steps(xb_ref, w13_ref, w2_ref, acc_ref)

    def norm():
        x1 = _layer_norm(ALPHA * x_ref[...] + 0.5 * acc_ref[...], lng_ref[...], lnb_ref[...])
        x1_ref[...] = x1
        xb_ref[...] = x1.astype(BF16)

    def proj(off, width, dtype, out_ref, keep_ref):
        def run():
            val = _dot(xb_ref[...], win_ref[:, off:off + width]).astype(dtype)
            if out_ref is not None:
                out_ref[...] = val
            if keep_ref is not None:
                keep_ref[cur] = val
        return run

    pre.append(norm)
    off = 0
    for out_ref, keep_ref, width, dtype in (
            (qa_ref, None, ATTN_WIDTH, BF16), (kv_ref, None, 2 * KV_WIDTH, BF16),
            (qh_ref, qbuf_ref, HG_WIDTH, BF16), (ih_ref, ibuf_ref, HG_WIDTH, BF16),
            (gh_ref, None, HG_WIDTH, BF16), (fb_ref, None, HG_WIDTH, F32),
            (None, fbuf_ref, HG_WIDTH, F32)):
        pre.append(proj(off, width, dtype, out_ref, keep_ref))
        off += width

    for step in _interleave(pre, scan):
        step()
    for h in range(HG_HEADS):
        st_ref[h] = state[h]


def _fwd_call(x, w13, w2, lng, lnb, win, lbf, tm, tiles_per_seq):
    m = x.shape[0]
    n = m // tm
    cur = lambda w: pl.BlockSpec((tm, w), lambda s: (jnp.minimum(s, n - 1), 0))
    prev = lambda w: pl.BlockSpec((tm, w), lambda s: (jnp.maximum(s - 1, 0), 0))
    widths = (D_MODEL, ATTN_WIDTH, 2 * KV_WIDTH, HG_WIDTH, HG_WIDTH, HG_WIDTH, HG_WIDTH)
    dtypes = (F32, BF16, BF16, BF16, BF16, BF16, F32)
    out_shape = [jax.ShapeDtypeStruct((m, w), d) for w, d in zip(widths, dtypes)]
    out_shape.append(jax.ShapeDtypeStruct((m, HG_WIDTH), F32))
    kern = functools.partial(_fwd_kernel, tiles_per_seq=tiles_per_seq)
    return pl.pallas_call(
        kern,
        grid=(n + 1,),
        in_specs=[cur(D_MODEL), _resident(w13.shape), _resident(w2.shape),
                  _resident(lng.shape), _resident(lnb.shape), _resident(win.shape),
                  _resident(lbf.shape)],
        out_specs=[cur(w) for w in widths] + [prev(HG_WIDTH)],
        out_shape=out_shape,
        scratch_shapes=[pltpu.VMEM((2, tm, HG_WIDTH), BF16), pltpu.VMEM((2, tm, HG_WIDTH), BF16),
                        pltpu.VMEM((2, tm, HG_WIDTH), F32),
                        pltpu.VMEM((HG_HEADS, LANES, HG_DK), F32),
                        pltpu.VMEM((tm, D_MODEL), BF16), pltpu.VMEM((tm, D_MODEL), F32)],
        compiler_params=pltpu.CompilerParams(
            dimension_semantics=("arbitrary",), vmem_limit_bytes=VMEM_LIMIT),
        name="fwd",
    )(x, w13, w2, lng, lnb, win, lbf)


def _bwd_kernel(qh_ref, ih_ref, fb_ref, lbb_ref, x1_ref, oa_ref, of_ref, gh_ref, hg_ref,
                wout_ref, ln2g_ref, ln2b_ref, w13_ref, w2_ref, ln3g_ref, ln3b_ref,
                out_ref, ob_ref, st_ref, xb_ref, acc_ref, x2_ref, *, tiles_per_seq):
    s = pl.program_id(0)
    cur = lax.rem(s, 2)
    prv = 1 - cur

    @pl.when(s == 0)
    def _():
        ob_ref[...] = jnp.zeros(ob_ref.shape, F32)
        st_ref[...] = jnp.zeros(st_ref.shape, F32)

    def mix():
        o = of_ref[...] + ob_ref[prv]
        parts = []
        for h in range(HG_HEADS):
            oh = o[:, h * LANES:(h + 1) * LANES]
            ms = jnp.mean(oh * oh, axis=1, keepdims=True)
            parts.append(oh * lax.rsqrt(ms + EPS))
        gate = gh_ref[...].astype(F32)
        ohg = jnp.concatenate(parts, axis=1) * hg_ref[...] * (gate * jax.nn.sigmoid(gate))
        y = (_dot(oa_ref[...], wout_ref[:ATTN_WIDTH, :])
             + _dot(ohg.astype(BF16), wout_ref[ATTN_WIDTH:, :]))
        x2 = _layer_norm(ALPHA * x1_ref[...] + y, ln2g_ref[...], ln2b_ref[...])
        x2_ref[...] = x2
        xb_ref[...] = x2.astype(BF16)

    def final():
        out_ref[...] = _layer_norm(ALPHA * x2_ref[...] + 0.5 * acc_ref[...],
                                   ln3g_ref[...], ln3b_ref[...])

    post = [mix] + _ffn_steps(xb_ref, w13_ref, w2_ref, acc_ref) + [final]

    restart = lax.rem(s, tiles_per_seq) == 0
    state = [jnp.where(restart, 0.0, st_ref[h]) for h in range(HG_HEADS)]
    scan = _scan_steps(qh_ref, ih_ref, fb_ref, _lower_bound(lbb_ref), state,
                       ob_ref.at[cur], True)
    for step in _interleave(post, scan):
        step()
    for h in range(HG_HEADS):
        st_ref[h] = state[h]


def _bwd_call(qh, ih, fb, lbb, x1, oa, of, gh, hg, wout, ln2g, ln2b, w13, w2, ln3g, ln3b,
              tm, tiles_per_seq):
    m = x1.shape[0]
    n = m // tm
    nt = tiles_per_seq

    def tile(s):
        return (s // nt) * nt + (nt - 1 - s % nt)

    cur = lambda w: pl.BlockSpec((tm, w), lambda s: (tile(jnp.minimum(s, n - 1)), 0))
    prev = lambda w: pl.BlockSpec((tm, w), lambda s: (tile(jnp.maximum(s - 1, 0)), 0))
    kern = functools.partial(_bwd_kernel, tiles_per_seq=nt)
    return pl.pallas_call(
        kern,
        grid=(n + 1,),
        in_specs=[cur(HG_WIDTH), cur(HG_WIDTH), cur(HG_WIDTH), _resident(lbb.shape),
                  prev(D_MODEL), prev(ATTN_WIDTH), prev(HG_WIDTH), prev(HG_WIDTH),
                  _resident(hg.shape), _resident(wout.shape), _resident(ln2g.shape),
                  _resident(ln2b.shape), _resident(w13.shape), _resident(w2.shape),
                  _resident(ln3g.shape), _resident(ln3b.shape)],
        out_specs=prev(D_MODEL),
        out_shape=jax.ShapeDtypeStruct((m, D_MODEL), F32),
        scratch_shapes=[pltpu.VMEM((2, tm, HG_WIDTH), F32),
                        pltpu.VMEM((HG_HEADS, LANES, HG_DK), F32),
                        pltpu.VMEM((tm, D_MODEL), BF16), pltpu.VMEM((tm, D_MODEL), F32),
                        pltpu.VMEM((tm, D_MODEL), F32)],
        compiler_params=pltpu.CompilerParams(
            dimension_semantics=("arbitrary",), vmem_limit_bytes=VMEM_LIMIT),
        name="bwd",
    )(qh, ih, fb, lbb, x1, oa, of, gh, hg, wout, ln2g, ln2b, w13, w2, ln3g, ln3b)


def _pair_chunks(w13):
    d, n = w13.shape[0], D_FF // FFN_CHUNK
    gate = w13[:, :D_FF].reshape(d, n, 1, FFN_CHUNK)
    up = w13[:, D_FF:].reshape(d, n, 1, FFN_CHUNK)
    return jnp.concatenate([gate, up], axis=2).reshape(d, 2 * D_FF)


def _prep_weights(ln_g, ln_b, ffn_w13, ffn_w2, w_in, attn_sink, attn_norm_g,
                  hg_lb_fwd, hg_lb_bwd, hg_norm_g, w_out):
    l = 0
    sizes = (ATTN_WIDTH, KV_WIDTH, KV_WIDTH, HG_WIDTH, HG_WIDTH, HG_WIDTH, HG_WIDTH, HG_WIDTH)
    offs = [0]
    for s in sizes:
        offs.append(offs[-1] + s)
    cols = [w_in[l][:, offs[i]:offs[i + 1]] for i in range(len(sizes))]
    q_a, k_a, v_a, q_h, f_f, f_b, i_h, g_h = cols
    win = jnp.concatenate([q_a * HEAD_DIM ** -0.5, k_a, v_a, q_h, i_h, g_h, f_b, f_f], axis=1)
    row = lambda a: a.reshape(1, -1).astype(F32)
    return dict(
        w13=[_pair_chunks(ffn_w13[l, i]).astype(BF16) for i in range(2)],
        w2=[ffn_w2[l, i].astype(BF16) for i in range(2)],
        lng=[row(ln_g[l, i]) for i in range(3)],
        lnb=[row(ln_b[l, i]) for i in range(3)],
        win=win.astype(BF16),
        sink=attn_sink[l].astype(F32),
        ng=row(attn_norm_g[l]),
        lbf=hg_lb_fwd.astype(F32),
        lbb=hg_lb_bwd.astype(F32),
        hg=row(hg_norm_g[l]),
        wout=w_out[l].astype(BF16),
    )


def _largest_tile(n, cap, unit):
    t = min(cap, n)
    while n % t or t % unit:
        t -= unit
    return t


def _layer(x, w):
    bsz, t, d = x.shape
    m = bsz * t
    tm = _largest_tile(t, ROW_TILE, GLA_CHUNK)
    tq = _largest_tile(t, ATTN_TILE, WINDOW)
    x1, qa, kv, qh, ih, gh, fb, of = _fwd_call(
        x.reshape(m, d), w["w13"][0], w["w2"][0], w["lng"][0], w["lnb"][0], w["win"],
        w["lbf"], tm, t // tm)
    seq = lambda a: a.reshape(bsz, t, a.shape[-1])
    oa = _attn_call(w["sink"], seq(qa), seq(kv), w["ng"], tq)
    y = _bwd_call(qh, ih, fb, w["lbb"], x1, oa.reshape(m, ATTN_WIDTH), of, gh, w["hg"],
                  w["wout"], w["lng"][1], w["lnb"][1], w["w13"][1], w["w2"][1],
                  w["lng"][2], w["lnb"][2], tm, t // tm)
    return y.reshape(bsz, t, d)


def kernel(x_prompt, x_sample, ln_g, ln_b, ffn_w13, ffn_w2, w_in, attn_sink, attn_norm_g,
           hg_lb_fwd, hg_lb_bwd, hg_norm_g, w_out):
    w = _prep_weights(ln_g, ln_b, ffn_w13, ffn_w2, w_in, attn_sink, attn_norm_g,
                      hg_lb_fwd, hg_lb_bwd, hg_norm_g, w_out)
    return (_layer(x_prompt, w), _layer(x_sample, w))
```

```python
import functools

import jax
import jax.numpy as jnp
from jax import lax
from jax.experimental import pallas as pl
from jax.experimental.pallas import tpu as pltpu

F32 = jnp.float32
BF16 = jnp.bfloat16

D_MODEL = 1024
DEPTH = 1
ATTN_WIDTH = 512
HEAD_DIM = 64
N_HEADS = 8
N_KV_HEADS = 2
KV_WIDTH = N_KV_HEADS * HEAD_DIM
WINDOW = 128
HG_WIDTH = 512
HG_HEADS = 4
HG_DK = 128
D_FF = 2816
ALPHA = (2.0 * DEPTH) ** 0.25
EPS = 1e-5
NEG = -1e30

LANES = 128
VMEM_LIMIT = 56 * 1024 * 1024

ROW_TILE = 256
ATTN_TILE = 512
FFN_CHUNK = 256
GLA_CHUNK = 128
GLA_BLK = 16
GLA_SUB = 4


def _dot(a, b):
    return jnp.dot(a, b, preferred_element_type=F32)


def _dot_nt(a, b):
    return lax.dot_general(a, b, (((1,), (1,)), ((), ())), preferred_element_type=F32)


def _dot_tn(a, b):
    return lax.dot_general(a, b, (((0,), (0,)), ((), ())), preferred_element_type=F32)


def _layer_norm(y, g, b):
    mu = jnp.mean(y, axis=-1, keepdims=True)
    yc = y - mu
    var = jnp.mean(yc * yc, axis=-1, keepdims=True)
    return yc * lax.rsqrt(var + EPS) * g + b


def _ffn_steps(xb_ref, w13_ref, w2_ref, acc_ref):
    fc = FFN_CHUNK
    n = D_FF // fc
    hidden = {}

    def step(kk):
        def run():
            if kk < n:
                xb = xb_ref[...]
                hidden[kk] = (_dot(xb, w13_ref[:, fc * kk:fc * (kk + 1)]),
                              _dot(xb, w13_ref[:, D_FF + fc * kk:D_FF + fc * (kk + 1)]))
            if kk > 0:
                gate, up = hidden.pop(kk - 1)
                act = (gate * jax.nn.sigmoid(gate)) * up
                part = _dot(act.astype(BF16), w2_ref[fc * (kk - 1):fc * kk, :])
                if kk == 1:
                    acc_ref[...] = part
                else:
                    acc_ref[...] += part
        return run

    return [step(kk) for kk in range(n + 1)]


def _interleave(major, minor):
    out, j = [], 0
    for i, step in enumerate(major):
        out.append(step)
        while j < len(minor) and (j + 1) * len(major) <= (i + 1) * len(minor):
            out.append(minor[j])
            j += 1
    return out + minor[j:]


def _resident(shape):
    nd = len(shape)
    return pl.BlockSpec(shape, lambda *_: (0,) * nd, pipeline_mode=pl.Buffered(1))


def _attn_kernel(sink_ref, q_ref, kvp_ref, kvc_ref, kvn_ref, ng_ref, o_ref,
                 kx_ref, vx_ref, bias_ref, *, tq):
    b_id = pl.program_id(0)
    i_id = pl.program_id(1)
    n_sub = tq // WINDOW
    span = 3 * WINDOW

    @pl.when((b_id == 0) & (i_id == 0))
    def _():
        r = lax.broadcasted_iota(jnp.int32, (WINDOW, span), 0)
        c = lax.broadcasted_iota(jnp.int32, (WINDOW, span), 1)
        dist = jnp.abs(c - WINDOW - r).astype(F32)
        for h in range(N_HEADS):
            slope = 2.0 ** (-8.0 * (h + 1) / N_HEADS)
            bias_ref[h] = jnp.where(dist <= float(WINDOW), -slope * dist, NEG)

    for lo_row, ref, n in ((0, kvp_ref, WINDOW), (WINDOW, kvc_ref, tq),
                           (WINDOW + tq, kvn_ref, WINDOW)):
        kv = ref[0]
        low = lax.broadcasted_iota(jnp.int32, (n, LANES), 1) < HEAD_DIM
        zero = jnp.zeros((n, LANES), BF16)
        for x, dst in ((kv[:, :LANES], kx_ref), (kv[:, LANES:], vx_ref)):
            xr = jnp.concatenate([x[:, HEAD_DIM:], x[:, :HEAD_DIM]], axis=1)
            rows = pl.ds(lo_row, n)
            dst[0, rows, :] = jnp.where(low, x, zero)
            dst[1, rows, :] = jnp.where(low, zero, xr)
            dst[2, rows, :] = jnp.where(low, xr, zero)
            dst[3, rows, :] = jnp.where(low, zero, x)

    col = lax.broadcasted_iota(jnp.int32, (WINDOW, span), 1)
    pen_lo = jnp.where(i_id == 0, NEG, 0.0).astype(F32)
    pen_hi = jnp.where(i_id == pl.num_programs(1) - 1, NEG, 0.0).astype(F32)
    low_out = lax.broadcasted_iota(jnp.int32, (WINDOW, LANES), 1) < HEAD_DIM
    gain = ng_ref[...]

    def block_steps(j):
        rows = slice(j * WINDOW, (j + 1) * WINDOW)
        srows = slice(j * WINDOW, j * WINDOW + span)
        logit = {}
        for g in range(N_KV_HEADS):
            qs = jnp.concatenate(
                [q_ref[0, rows, g * 256:g * 256 + LANES],
                 q_ref[0, rows, g * 256 + LANES:(g + 1) * 256]], axis=0)
            for e in range(2):
                logit[g, e] = _dot_nt(qs, kx_ref[2 * g + e, srows, :])
        yield

        probs, dens = {}, {}
        for g in range(N_KV_HEADS):
            for r in range(2):
                for e in range(2):
                    h = 4 * g + 2 * r + e
                    l = logit[g, e][r * WINDOW:(r + 1) * WINDOW, :] + bias_ref[h]
                    if j == 0:
                        l = l + jnp.where(col < WINDOW, pen_lo, 0.0)
                    if j == n_sub - 1:
                        l = l + jnp.where(col >= 2 * WINDOW, pen_hi, 0.0)
                    sk = sink_ref[h]
                    m = jnp.maximum(jnp.max(l, axis=1, keepdims=True), sk)
                    p = jnp.exp(l - m)
                    dens[h] = jnp.sum(p, axis=1, keepdims=True) + jnp.exp(sk - m)
                    probs[h] = p.astype(BF16)
        yield

        outs = []
        for g in range(N_KV_HEADS):
            for r in range(2):
                h = 4 * g + 2 * r
                acc = (_dot(probs[h], vx_ref[2 * g, srows, :])
                       + _dot(probs[h + 1], vx_ref[2 * g + 1, srows, :]))
                outs.append(acc * jnp.where(low_out, 1.0 / dens[h], 1.0 / dens[h + 1]))
        oj = jnp.concatenate(outs, axis=1)
        ms = jnp.mean(oj * oj, axis=1, keepdims=True)
        o_ref[0, rows, :] = (oj * lax.rsqrt(ms + EPS) * gain).astype(o_ref.dtype)

    _run_staggered([block_steps(j) for j in range(n_sub)])


def _attn_call(sink, qa, kv, ng, tq):
    bsz, t, _ = qa.shape
    r = tq // WINDOW
    nblk = t // WINDOW
    kern = functools.partial(_attn_kernel, tq=tq)
    return pl.pallas_call(
        kern,
        grid=(bsz, t // tq),
        in_specs=[
            pl.BlockSpec(memory_space=pltpu.SMEM),
            pl.BlockSpec((1, tq, ATTN_WIDTH), lambda b, i: (b, i, 0)),
            pl.BlockSpec((1, WINDOW, 2 * KV_WIDTH), lambda b, i: (b, jnp.maximum(i * r - 1, 0), 0)),
            pl.BlockSpec((1, tq, 2 * KV_WIDTH), lambda b, i: (b, i, 0)),
            pl.BlockSpec((1, WINDOW, 2 * KV_WIDTH),
                         lambda b, i: (b, jnp.minimum((i + 1) * r, nblk - 1), 0)),
            pl.BlockSpec((1, ATTN_WIDTH), lambda b, i: (0, 0)),
        ],
        out_specs=pl.BlockSpec((1, tq, ATTN_WIDTH), lambda b, i: (b, i, 0)),
        out_shape=jax.ShapeDtypeStruct((bsz, t, ATTN_WIDTH), BF16),
        scratch_shapes=[
            pltpu.VMEM((4, tq + 2 * WINDOW, LANES), BF16),
            pltpu.VMEM((4, tq + 2 * WINDOW, LANES), BF16),
            pltpu.VMEM((N_HEADS, WINDOW, 3 * WINDOW), F32),
        ],
        compiler_params=pltpu.CompilerParams(
            dimension_semantics=("arbitrary", "arbitrary"), vmem_limit_bytes=VMEM_LIMIT),
        name="attn",
    )(sink, qa, kv, kv, kv, ng)


def _gla_consts(c, rev):
    ti = lax.broadcasted_iota(jnp.int32, (c, c), 0)
    si = lax.broadcasted_iota(jnp.int32, (c, c), 1)
    d = (si - ti) if rev else (ti - si)
    cum = jnp.where(d >= 0, 1.0, 0.0).astype(BF16)
    same_blk = (ti // GLA_BLK) == (si // GLA_BLK)
    same_sub = (ti // GLA_SUB) == (si // GLA_SUB)
    code = jnp.where(same_sub, d, -1)
    row = lax.broadcasted_iota(jnp.int32, (c, HG_WIDTH), 0)
    return dict(cum=cum, same_blk=same_blk, code=code,
                sub=(row // GLA_SUB) % (GLA_BLK // GLA_SUB))


def _bcast_rows(b, rows, n):
    return jnp.concatenate(
        [jnp.broadcast_to(b[r:r + 1, :], (n, b.shape[1])) for r in rows], axis=0)


def _stack_rows(parts):
    return jnp.concatenate([p for p in parts if p.shape[0]], axis=0)


def _head(x, h):
    return x[:, h * LANES:(h + 1) * LANES]


def _pair(x, p):
    return x[:, 2 * p * LANES:2 * (p + 1) * LANES]


def _block_diag(x0, x1):
    top = jnp.concatenate([x0, jnp.zeros((x0.shape[0], x1.shape[1]), x0.dtype)], axis=1)
    bot = jnp.concatenate([jnp.zeros((x1.shape[0], x0.shape[1]), x1.dtype), x1], axis=1)
    return jnp.concatenate([top, bot], axis=0)


def _gla_chunk_steps(q_ref, v_ref, f_ref, lb, state, emit, cst, rev):
    q, v, flog = q_ref[...], v_ref[...], f_ref[...]
    c, width = flog.shape
    nb = c // GLA_BLK
    ns = GLA_BLK // GLA_SUB
    pairs = range(width // (2 * LANES))
    qf = q.astype(F32)
    s = jax.nn.sigmoid(flog)
    g = jnp.log2(lb + (1.0 - lb) * s)
    k = (1.0 - lb) * (1.0 - s)
    kb = k.astype(BF16)
    g1 = g.astype(BF16)
    g2 = (g - g1.astype(F32)).astype(BF16)
    gsplit = jnp.concatenate([g1, g2], axis=1)

    win = g
    near = [q]
    for dist in range(1, GLA_SUB):
        near.append((qf * jnp.exp2(win)).astype(BF16))
        if dist < GLA_SUB - 1:
            win = win + pltpu.roll(g, (c - dist) if rev else dist, 0)
    near = jnp.concatenate(near, axis=0)
    yield

    b2 = _dot(cst["cum"], gsplit)
    b = b2[:, :width] + b2[:, width:]
    p3 = [_dot_nt(_pair(near, p), _block_diag(_head(kb, 2 * p), _head(kb, 2 * p + 1)))
          for p in pairs]
    sub = cst["sub"]
    blk_end = [GLA_BLK * j + (0 if rev else GLA_BLK - 1) for j in range(nb)]
    sub_end = [[GLA_BLK * j + GLA_SUB * u + (0 if rev else GLA_SUB - 1) for j in range(nb)]
               for u in range(ns)]

    src_blocks = range(1, nb) if rev else range(nb - 1)
    e_blk = _bcast_rows(b, blk_end, GLA_BLK)
    kt = (k * jnp.exp2(e_blk - b)).astype(BF16)
    lhs1, rhs1 = [], []
    for j in src_blocks:
        lo, hi = GLA_BLK * j, GLA_BLK * (j + 1)
        later = slice(0, lo) if rev else slice(hi, c)
        e = b[blk_end[j]:blk_end[j] + 1, :]
        part = (qf[later] * jnp.exp2(b[later] - e)).astype(BF16)
        pad = jnp.zeros((c - part.shape[0], width), BF16)
        lhs1.append(jnp.concatenate([part, pad] if rev else [pad, part], axis=0))
        rhs1.append(_stack_rows([jnp.zeros((lo, width), BF16), kt[lo:hi],
                                 jnp.zeros((c - hi, width), BF16)]))

    e_sub = [_bcast_rows(b, sub_end[u], GLA_BLK) for u in range(ns)]
    e_own = e_sub[ns - 1]
    for u in range(ns - 2, -1, -1):
        e_own = jnp.where(sub == u, e_sub[u], e_own)
    kt2 = k * jnp.exp2(e_own - b)
    src_subs = range(1, ns) if rev else range(ns - 1)
    lhs2, rhs2 = [], []
    for u in src_subs:
        later = (sub < u) if rev else (sub > u)
        lhs2.append((qf * jnp.exp2(jnp.where(later, b - e_sub[u], NEG))).astype(BF16))
        rhs2.append(jnp.where(sub == u, kt2, 0.0).astype(BF16))

    e_end = b[0:1, :] if rev else b[c - 1:c, :]
    qd = (qf * jnp.exp2(b)).astype(BF16)
    kdec = (k * jnp.exp2(e_end - b)).astype(BF16)
    dec = jnp.exp2(e_end)
    yield

    a = []
    for p in pairs:
        h0, h1 = 2 * p, 2 * p + 1
        cat = lambda xs, h: jnp.concatenate([_head(x, h) for x in xs], axis=1)
        a1 = _dot_nt(jnp.concatenate([cat(lhs1, h0), cat(lhs1, h1)], axis=1),
                     _block_diag(cat(rhs1, h0), cat(rhs1, h1)))
        a2 = _dot_nt(jnp.concatenate([cat(lhs2, h0), cat(lhs2, h1)], axis=1),
                     _block_diag(cat(rhs2, h0), cat(rhs2, h1)))
        for e in range(2):
            cols = slice(e * c, (e + 1) * c)
            ah = a1[:, cols] + jnp.where(cst["same_blk"], a2[:, cols], 0.0)
            for dist in range(GLA_SUB):
                ah = jnp.where(cst["code"] == dist, p3[p][dist * c:(dist + 1) * c, cols], ah)
            a.append(ah.astype(BF16))
    yield

    outs = []
    for p in pairs:
        h0, h1 = 2 * p, 2 * p + 1
        st0, st1 = state[h0], state[h1]
        intra = _dot(jnp.concatenate([a[h0], a[h1]], axis=1),
                     _block_diag(_head(v, h0), _head(v, h1)))
        inter = _dot_nt(_pair(qd, p), _block_diag(st0.astype(BF16), st1.astype(BF16)))
        outs.append(intra + inter)
        state[h0] = st0 * _head(dec, h0) + _dot_tn(_head(v, h0), _head(kdec, h0))
        state[h1] = st1 * _head(dec, h1) + _dot_tn(_head(v, h1), _head(kdec, h1))
    emit(jnp.concatenate(outs, axis=1))


def _lower_bound(lb_ref):
    p = lb_ref[...]
    e = jnp.exp(p - jnp.max(p, axis=0, keepdims=True))
    return e[0:1, :] / jnp.sum(e, axis=0, keepdims=True)


def _run_staggered(gens):
    done = [False] * len(gens)
    rounds = 0
    while not all(done):
        for i, gen in enumerate(gens):
            if i <= rounds and not done[i]:
                try:
                    next(gen)
                except StopIteration:
                    done[i] = True
        rounds += 1


def _scan_steps(q_ref, v_ref, f_ref, lb, state, o_ref, rev):
    n = q_ref.shape[0] // GLA_CHUNK
    cst = _gla_consts(GLA_CHUNK, rev)
    gens = []
    for ci in (range(n - 1, -1, -1) if rev else range(n)):
        rows = slice(ci * GLA_CHUNK, (ci + 1) * GLA_CHUNK)

        def emit(o, rows=rows):
            o_ref[rows, :] = o

        gens.append(_gla_chunk_steps(q_ref.at[rows, :], v_ref.at[rows, :], f_ref.at[rows, :],
                                     lb, state, emit, cst, rev))
    steps = []

    def advance(gen):
        def run():
            next(gen, None)
        return run

    depth = 4
    for rnd in range(len(gens) + depth - 1):
        for i, gen in enumerate(gens):
            if 0 <= rnd - i < depth:
                steps.append(advance(gen))
    return steps


def _fwd_kernel(x_ref, w13_ref, w2_ref, lng_ref, lnb_ref, win_ref, lbf_ref,
                x1_ref, qa_ref, kv_ref, qh_ref, ih_ref, gh_ref, fb_ref, of_ref,
                qbuf_ref, ibuf_ref, fbuf_ref, st_ref, xb_ref, acc_ref,
                *, tiles_per_seq):
    s = pl.program_id(0)
    cur = lax.rem(s, 2)
    prv = 1 - cur

    @pl.when(s == 0)
    def _():
        qbuf_ref[...] = jnp.zeros(qbuf_ref.shape, BF16)
        ibuf_ref[...] = jnp.zeros(ibuf_ref.shape, BF16)
        fbuf_ref[...] = jnp.zeros(fbuf_ref.shape, F32)
        st_ref[...] = jnp.zeros(st_ref.shape, F32)

    restart = lax.rem(s - 1, tiles_per_seq) == 0
    state = [jnp.where(restart, 0.0, st_ref[h]) for h in range(HG_HEADS)]
    scan = _scan_steps(qbuf_ref.at[prv], ibuf_ref.at[prv], fbuf_ref.at[prv],
                       _lower_bound(lbf_ref), state, of_ref, False)

    xb_ref[...] = x_ref[...].astype(BF16)
    pre = _ffn_steps(xb_ref, w13_ref, w2_ref, acc_ref)

    def norm():
        x1 = _layer_norm(ALPHA * x_ref[...] + 0.5 * acc_ref[...], lng_ref[...], lnb_ref[...])
        x1_ref[...] = x1
        xb_ref[...] = x1.astype(BF16)

    def proj(off, width, dtype, out_ref, keep_ref):
        def run():
            val = _dot(xb_ref[...], win_ref[:, off:off + width]).astype(dtype)
            if out_ref is not None:
                out_ref[...] = val
            if keep_ref is not None:
                keep_ref[cur] = val
        return run

    pre.append(norm)
    off = 0
    for out_ref, keep_ref, width, dtype in (
            (qa_ref, None, ATTN_WIDTH, BF16), (kv_ref, None, 2 * KV_WIDTH, BF16),
            (qh_ref, qbuf_ref, HG_WIDTH, BF16), (ih_ref, ibuf_ref, HG_WIDTH, BF16),
            (gh_ref, None, HG_WIDTH, BF16), (fb_ref, None, HG_WIDTH, F32),
            (None, fbuf_ref, HG_WIDTH, F32)):
        pre.append(proj(off, width, dtype, out_ref, keep_ref))
        off += width

    for step in _interleave(pre, scan):
        step()
    for h in range(HG_HEADS):
        st_ref[h] = state[h]


def _fwd_call(x, w13, w2, lng, lnb, win, lbf, tm, tiles_per_seq):
    m = x.shape[0]
    n = m // tm
    cur = lambda w: pl.BlockSpec((tm, w), lambda s: (jnp.minimum(s, n - 1), 0))
    prev = lambda w: pl.BlockSpec((tm, w), lambda s: (jnp.maximum(s - 1, 0), 0))
    widths = (D_MODEL, ATTN_WIDTH, 2 * KV_WIDTH, HG_WIDTH, HG_WIDTH, HG_WIDTH, HG_WIDTH)
    dtypes = (F32, BF16, BF16, BF16, BF16, BF16, F32)
    out_shape = [jax.ShapeDtypeStruct((m, w), d) for w, d in zip(widths, dtypes)]
    out_shape.append(jax.ShapeDtypeStruct((m, HG_WIDTH), F32))
    kern = functools.partial(_fwd_kernel, tiles_per_seq=tiles_per_seq)
    return pl.pallas_call(
        kern,
        grid=(n + 1,),
        in_specs=[cur(D_MODEL), _resident(w13.shape), _resident(w2.shape),
                  _resident(lng.shape), _resident(lnb.shape), _resident(win.shape),
                  _resident(lbf.shape)],
        out_specs=[cur(w) for w in widths] + [prev(HG_WIDTH)],
        out_shape=out_shape,
        scratch_shapes=[pltpu.VMEM((2, tm, HG_WIDTH), BF16), pltpu.VMEM((2, tm, HG_WIDTH), BF16),
                        pltpu.VMEM((2, tm, HG_WIDTH), F32),
                        pltpu.VMEM((HG_HEADS, LANES, HG_DK), F32),
                        pltpu.VMEM((tm, D_MODEL), BF16), pltpu.VMEM((tm, D_MODEL), F32)],
        compiler_params=pltpu.CompilerParams(
            dimension_semantics=("arbitrary",), vmem_limit_bytes=VMEM_LIMIT),
        name="fwd",
    )(x, w13, w2, lng, lnb, win, lbf)


def _bwd_kernel(qh_ref, ih_ref, fb_ref, lbb_ref, x1_ref, oa_ref, of_ref, gh_ref, hg_ref,
                wout_ref, ln2g_ref, ln2b_ref, w13_ref, w2_ref, ln3g_ref, ln3b_ref,
                out_ref, ob_ref, st_ref, xb_ref, acc_ref, x2_ref, *, tiles_per_seq):
    s = pl.program_id(0)
    cur = lax.rem(s, 2)
    prv = 1 - cur

    @pl.when(s == 0)
    def _():
        ob_ref[...] = jnp.zeros(ob_ref.shape, F32)
        st_ref[...] = jnp.zeros(st_ref.shape, F32)

    def mix():
        o = of_ref[...] + ob_ref[prv]
        parts = []
        for h in range(HG_HEADS):
            oh = o[:, h * LANES:(h + 1) * LANES]
            ms = jnp.mean(oh * oh, axis=1, keepdims=True)
            parts.append(oh * lax.rsqrt(ms + EPS))
        gate = gh_ref[...].astype(F32)
        ohg = jnp.concatenate(parts, axis=1) * hg_ref[...] * (gate * jax.nn.sigmoid(gate))
        y = (_dot(oa_ref[...], wout_ref[:ATTN_WIDTH, :])
             + _dot(ohg.astype(BF16), wout_ref[ATTN_WIDTH:, :]))
        x2 = _layer_norm(ALPHA * x1_ref[...] + y, ln2g_ref[...], ln2b_ref[...])
        x2_ref[...] = x2
        xb_ref[...] = x2.astype(BF16)

    def final():
        out_ref[...] = _layer_norm(ALPHA * x2_ref[...] + 0.5 * acc_ref[...],
                                   ln3g_ref[...], ln3b_ref[...])

    post = [mix] + _ffn_steps(xb_ref, w13_ref, w2_ref, acc_ref) + [final]

    restart = lax.rem(s, tiles_per_seq) == 0
    state = [jnp.where(restart, 0.0, st_ref[h]) for h in range(HG_HEADS)]
    scan = _scan_steps(qh_ref, ih_ref, fb_ref, _lower_bound(lbb_ref), state,
                       ob_ref.at[cur], True)
    for step in _interleave(post, scan):
        step()
    for h in range(HG_HEADS):
        st_ref[h] = state[h]


def _bwd_call(qh, ih, fb, lbb, x1, oa, of, gh, hg, wout, ln2g, ln2b, w13, w2, ln3g, ln3b,
              tm, tiles_per_seq):
    m = x1.shape[0]
    n = m // tm
    nt = tiles_per_seq

    def tile(s):
        return (s // nt) * nt + (nt - 1 - s % nt)

    cur = lambda w: pl.BlockSpec((tm, w), lambda s: (tile(jnp.minimum(s, n - 1)), 0))
    prev = lambda w: pl.BlockSpec((tm, w), lambda s: (tile(jnp.maximum(s - 1, 0)), 0))
    kern = functools.partial(_bwd_kernel, tiles_per_seq=nt)
    return pl.pallas_call(
        kern,
        grid=(n + 1,),
        in_specs=[cur(HG_WIDTH), cur(HG_WIDTH), cur(HG_WIDTH), _resident(lbb.shape),
                  prev(D_MODEL), prev(ATTN_WIDTH), prev(HG_WIDTH), prev(HG_WIDTH),
                  _resident(hg.shape), _resident(wout.shape), _resident(ln2g.shape),
                  _resident(ln2b.shape), _resident(w13.shape), _resident(w2.shape),
                  _resident(ln3g.shape), _resident(ln3b.shape)],
        out_specs=prev(D_MODEL),
        out_shape=jax.ShapeDtypeStruct((m, D_MODEL), F32),
        scratch_shapes=[pltpu.VMEM((2, tm, HG_WIDTH), F32),
                        pltpu.VMEM((HG_HEADS, LANES, HG_DK), F32),
                        pltpu.VMEM((tm, D_MODEL), BF16), pltpu.VMEM((tm, D_MODEL), F32),
                        pltpu.VMEM((tm, D_MODEL), F32)],
        compiler_params=pltpu.CompilerParams(
            dimension_semantics=("arbitrary",), vmem_limit_bytes=VMEM_LIMIT),
        name="bwd",
    )(qh, ih, fb, lbb, x1, oa, of, gh, hg, wout, ln2g, ln2b, w13, w2, ln3g, ln3b)


def _prep_weights(ln_g, ln_b, ffn_w13, ffn_w2, w_in, attn_sink, attn_norm_g,
                  hg_lb_fwd, hg_lb_bwd, hg_norm_g, w_out):
    l = 0
    sizes = (ATTN_WIDTH, KV_WIDTH, KV_WIDTH, HG_WIDTH, HG_WIDTH, HG_WIDTH, HG_WIDTH, HG_WIDTH)
    offs = [0]
    for s in sizes:
        offs.append(offs[-1] + s)
    cols = [w_in[l][:, offs[i]:offs[i + 1]] for i in range(len(sizes))]
    q_a, k_a, v_a, q_h, f_f, f_b, i_h, g_h = cols
    win = jnp.concatenate([q_a * HEAD_DIM ** -0.5, k_a, v_a, q_h, i_h, g_h, f_b, f_f], axis=1)
    row = lambda a: a.reshape(1, -1).astype(F32)
    return dict(
        w13=[ffn_w13[l, i].astype(BF16) for i in range(2)],
        w2=[ffn_w2[l, i].astype(BF16) for i in range(2)],
        lng=[row(ln_g[l, i]) for i in range(3)],
        lnb=[row(ln_b[l, i]) for i in range(3)],
        win=win.astype(BF16),
        sink=attn_sink[l].astype(F32),
        ng=row(attn_norm_g[l]),
        lbf=hg_lb_fwd.astype(F32),
        lbb=hg_lb_bwd.astype(F32),
        hg=row(hg_norm_g[l]),
        wout=w_out[l].astype(BF16),
    )


def _largest_tile(n, cap, unit):
    t = min(cap, n)
    while n % t or t % unit:
        t -= unit
    return t


def _layer(x, w):
    bsz, t, d = x.shape
    m = bsz * t
    tm = _largest_tile(t, ROW_TILE, GLA_CHUNK)
    tq = _largest_tile(t, ATTN_TILE, WINDOW)
    x1, qa, kv, qh, ih, gh, fb, of = _fwd_call(
        x.reshape(m, d), w["w13"][0], w["w2"][0], w["lng"][0], w["lnb"][0], w["win"],
        w["lbf"], tm, t // tm)
    seq = lambda a: a.reshape(bsz, t, a.shape[-1])
    oa = _attn_call(w["sink"], seq(qa), seq(kv), w["ng"], tq)
    y = _bwd_call(qh, ih, fb, w["lbb"], x1, oa.reshape(m, ATTN_WIDTH), of, gh, w["hg"],
                  w["wout"], w["lng"][1], w["lnb"][1], w["w13"][1], w["w2"][1],
                  w["lng"][2], w["lnb"][2], tm, t // tm)
    return y.reshape(bsz, t, d)


def kernel(x_prompt, x_sample, ln_g, ln_b, ffn_w13, ffn_w2, w_in, attn_sink, attn_norm_g,
           hg_lb_fwd, hg_lb_bwd, hg_norm_g, w_out):
    w = _prep_weights(ln_g, ln_b, ffn_w13, ffn_w2, w_in, attn_sink, attn_norm_g,
                      hg_lb_fwd, hg_lb_bwd, hg_norm_g, w_out)
    return (_layer(x_prompt, w), _layer(x_sample, w))
```

```python
import functools

import jax
import jax.numpy as jnp
from jax import lax
from jax.experimental import pallas as pl
from jax.experimental.pallas import tpu as pltpu

F32 = jnp.float32
BF16 = jnp.bfloat16

D_MODEL = 1024
DEPTH = 1
ATTN_WIDTH = 512
HEAD_DIM = 64
N_HEADS = 8
N_KV_HEADS = 2
KV_WIDTH = N_KV_HEADS * HEAD_DIM
WINDOW = 128
HG_WIDTH = 512
HG_HEADS = 4
HG_DK = 128
D_FF = 2816
ALPHA = (2.0 * DEPTH) ** 0.25
EPS = 1e-5
NEG = -1e30

LANES = 128
VMEM_LIMIT = 56 * 1024 * 1024

ROW_TILE = 512
ATTN_TILE = 512
FFN_CHUNK = 256
GLA_CHUNK = 128
GLA_BLK = 16
GLA_SUB = 4


def _dot(a, b):
    return jnp.dot(a, b, preferred_element_type=F32)


def _dot_nt(a, b):
    return lax.dot_general(a, b, (((1,), (1,)), ((), ())), preferred_element_type=F32)


def _dot_tn(a, b):
    return lax.dot_general(a, b, (((0,), (0,)), ((), ())), preferred_element_type=F32)


def _layer_norm(y, g, b):
    mu = jnp.mean(y, axis=-1, keepdims=True)
    yc = y - mu
    var = jnp.mean(yc * yc, axis=-1, keepdims=True)
    return yc * lax.rsqrt(var + EPS) * g + b


def _ffn_steps(xb_ref, w13_ref, w2_ref, acc_ref):
    fc = FFN_CHUNK
    n = D_FF // fc
    hidden = {}

    def step(kk):
        def run():
            if kk < n:
                xb = xb_ref[...]
                hidden[kk] = (_dot(xb, w13_ref[:, fc * kk:fc * (kk + 1)]),
                              _dot(xb, w13_ref[:, D_FF + fc * kk:D_FF + fc * (kk + 1)]))
            if kk > 0:
                gate, up = hidden.pop(kk - 1)
                act = (gate * jax.nn.sigmoid(gate)) * up
                part = _dot(act.astype(BF16), w2_ref[fc * (kk - 1):fc * kk, :])
                if kk == 1:
                    acc_ref[...] = part
                else:
                    acc_ref[...] += part
        return run

    return [step(kk) for kk in range(n + 1)]


def _interleave(major, minor):
    out, j = [], 0
    for i, step in enumerate(major):
        out.append(step)
        while j < len(minor) and (j + 1) * len(major) <= (i + 1) * len(minor):
            out.append(minor[j])
            j += 1
    return out + minor[j:]


def _resident(shape):
    nd = len(shape)
    return pl.BlockSpec(shape, lambda *_: (0,) * nd, pipeline_mode=pl.Buffered(1))


def _attn_kernel(sink_ref, q_ref, kvp_ref, kvc_ref, kvn_ref, ng_ref, o_ref,
                 kx_ref, vx_ref, bias_ref, *, tq):
    b_id = pl.program_id(0)
    i_id = pl.program_id(1)
    n_sub = tq // WINDOW
    span = 3 * WINDOW

    @pl.when((b_id == 0) & (i_id == 0))
    def _():
        r = lax.broadcasted_iota(jnp.int32, (WINDOW, span), 0)
        c = lax.broadcasted_iota(jnp.int32, (WINDOW, span), 1)
        dist = jnp.abs(c - WINDOW - r).astype(F32)
        for h in range(N_HEADS):
            slope = 2.0 ** (-8.0 * (h + 1) / N_HEADS)
            bias_ref[h] = jnp.where(dist <= float(WINDOW), -slope * dist, NEG)

    for lo_row, ref, n in ((0, kvp_ref, WINDOW), (WINDOW, kvc_ref, tq),
                           (WINDOW + tq, kvn_ref, WINDOW)):
        kv = ref[0]
        low = lax.broadcasted_iota(jnp.int32, (n, LANES), 1) < HEAD_DIM
        zero = jnp.zeros((n, LANES), BF16)
        for x, dst in ((kv[:, :LANES], kx_ref), (kv[:, LANES:], vx_ref)):
            xr = jnp.concatenate([x[:, HEAD_DIM:], x[:, :HEAD_DIM]], axis=1)
            rows = pl.ds(lo_row, n)
            dst[0, rows, :] = jnp.where(low, x, zero)
            dst[1, rows, :] = jnp.where(low, zero, xr)
            dst[2, rows, :] = jnp.where(low, xr, zero)
            dst[3, rows, :] = jnp.where(low, zero, x)

    col = lax.broadcasted_iota(jnp.int32, (WINDOW, span), 1)
    pen_lo = jnp.where(i_id == 0, NEG, 0.0).astype(F32)
    pen_hi = jnp.where(i_id == pl.num_programs(1) - 1, NEG, 0.0).astype(F32)
    low_out = lax.broadcasted_iota(jnp.int32, (WINDOW, LANES), 1) < HEAD_DIM
    gain = ng_ref[...]

    def block_steps(j):
        rows = slice(j * WINDOW, (j + 1) * WINDOW)
        srows = slice(j * WINDOW, j * WINDOW + span)
        logit = {}
        for g in range(N_KV_HEADS):
            qs = jnp.concatenate(
                [q_ref[0, rows, g * 256:g * 256 + LANES],
                 q_ref[0, rows, g * 256 + LANES:(g + 1) * 256]], axis=0)
            for e in range(2):
                logit[g, e] = _dot_nt(qs, kx_ref[2 * g + e, srows, :])
        yield

        probs, dens = {}, {}
        for g in range(N_KV_HEADS):
            for r in range(2):
                for e in range(2):
                    h = 4 * g + 2 * r + e
                    l = logit[g, e][r * WINDOW:(r + 1) * WINDOW, :] + bias_ref[h]
                    if j == 0:
                        l = l + jnp.where(col < WINDOW, pen_lo, 0.0)
                    if j == n_sub - 1:
                        l = l + jnp.where(col >= 2 * WINDOW, pen_hi, 0.0)
                    sk = sink_ref[h]
                    m = jnp.maximum(jnp.max(l, axis=1, keepdims=True), sk)
                    p = jnp.exp(l - m)
                    dens[h] = jnp.sum(p, axis=1, keepdims=True) + jnp.exp(sk - m)
                    probs[h] = p.astype(BF16)
        yield

        outs = []
        for g in range(N_KV_HEADS):
            for r in range(2):
                h = 4 * g + 2 * r
                acc = (_dot(probs[h], vx_ref[2 * g, srows, :])
                       + _dot(probs[h + 1], vx_ref[2 * g + 1, srows, :]))
                outs.append(acc * jnp.where(low_out, 1.0 / dens[h], 1.0 / dens[h + 1]))
        oj = jnp.concatenate(outs, axis=1)
        ms = jnp.mean(oj * oj, axis=1, keepdims=True)
        o_ref[0, rows, :] = (oj * lax.rsqrt(ms + EPS) * gain).astype(o_ref.dtype)

    _run_staggered([block_steps(j) for j in range(n_sub)])


def _attn_call(sink, qa, kv, ng, tq):
    bsz, t, _ = qa.shape
    r = tq // WINDOW
    nblk = t // WINDOW
    kern = functools.partial(_attn_kernel, tq=tq)
    return pl.pallas_call(
        kern,
        grid=(bsz, t // tq),
        in_specs=[
            pl.BlockSpec(memory_space=pltpu.SMEM),
            pl.BlockSpec((1, tq, ATTN_WIDTH), lambda b, i: (b, i, 0)),
            pl.BlockSpec((1, WINDOW, 2 * KV_WIDTH), lambda b, i: (b, jnp.maximum(i * r - 1, 0), 0)),
            pl.BlockSpec((1, tq, 2 * KV_WIDTH), lambda b, i: (b, i, 0)),
            pl.BlockSpec((1, WINDOW, 2 * KV_WIDTH),
                         lambda b, i: (b, jnp.minimum((i + 1) * r, nblk - 1), 0)),
            pl.BlockSpec((1, ATTN_WIDTH), lambda b, i: (0, 0)),
        ],
        out_specs=pl.BlockSpec((1, tq, ATTN_WIDTH), lambda b, i: (b, i, 0)),
        out_shape=jax.ShapeDtypeStruct((bsz, t, ATTN_WIDTH), BF16),
        scratch_shapes=[
            pltpu.VMEM((4, tq + 2 * WINDOW, LANES), BF16),
            pltpu.VMEM((4, tq + 2 * WINDOW, LANES), BF16),
            pltpu.VMEM((N_HEADS, WINDOW, 3 * WINDOW), F32),
        ],
        compiler_params=pltpu.CompilerParams(
            dimension_semantics=("arbitrary", "arbitrary"), vmem_limit_bytes=VMEM_LIMIT),
        name="attn",
    )(sink, qa, kv, kv, kv, ng)


def _gla_consts(c, rev):
    ti = lax.broadcasted_iota(jnp.int32, (c, c), 0)
    si = lax.broadcasted_iota(jnp.int32, (c, c), 1)
    d = (si - ti) if rev else (ti - si)
    cum = jnp.where(d >= 0, 1.0, 0.0).astype(BF16)
    same_blk = (ti // GLA_BLK) == (si // GLA_BLK)
    same_sub = (ti // GLA_SUB) == (si // GLA_SUB)
    code = jnp.where(same_sub, d, -1)
    row = lax.broadcasted_iota(jnp.int32, (c, HG_WIDTH), 0)
    return dict(cum=cum, same_blk=same_blk, code=code,
                sub=(row // GLA_SUB) % (GLA_BLK // GLA_SUB))


def _bcast_rows(b, rows, n):
    return jnp.concatenate(
        [jnp.broadcast_to(b[r:r + 1, :], (n, b.shape[1])) for r in rows], axis=0)


def _stack_rows(parts):
    return jnp.concatenate([p for p in parts if p.shape[0]], axis=0)


def _head(x, h):
    return x[:, h * LANES:(h + 1) * LANES]


def _pair(x, p):
    return x[:, 2 * p * LANES:2 * (p + 1) * LANES]


def _block_diag(x0, x1):
    top = jnp.concatenate([x0, jnp.zeros((x0.shape[0], x1.shape[1]), x0.dtype)], axis=1)
    bot = jnp.concatenate([jnp.zeros((x1.shape[0], x0.shape[1]), x1.dtype), x1], axis=1)
    return jnp.concatenate([top, bot], axis=0)


def _gla_chunk_steps(q_ref, v_ref, f_ref, lb, state, emit, cst, rev):
    q, v, flog = q_ref[...], v_ref[...], f_ref[...]
    c, width = flog.shape
    nb = c // GLA_BLK
    ns = GLA_BLK // GLA_SUB
    pairs = range(width // (2 * LANES))
    qf = q.astype(F32)
    s = jax.nn.sigmoid(flog)
    g = jnp.log2(lb + (1.0 - lb) * s)
    k = (1.0 - lb) * (1.0 - s)
    kb = k.astype(BF16)
    g1 = g.astype(BF16)
    g2 = (g - g1.astype(F32)).astype(BF16)
    gsplit = jnp.concatenate([g1, g2], axis=1)

    win = g
    near = [q]
    for dist in range(1, GLA_SUB):
        near.append((qf * jnp.exp2(win)).astype(BF16))
        if dist < GLA_SUB - 1:
            win = win + pltpu.roll(g, (c - dist) if rev else dist, 0)
    near = jnp.concatenate(near, axis=0)
    yield

    b2 = _dot(cst["cum"], gsplit)
    b = b2[:, :width] + b2[:, width:]
    p3 = [_dot_nt(_pair(near, p), _block_diag(_head(kb, 2 * p), _head(kb, 2 * p + 1)))
          for p in pairs]
    sub = cst["sub"]
    blk_end = [GLA_BLK * j + (0 if rev else GLA_BLK - 1) for j in range(nb)]
    sub_end = [[GLA_BLK * j + GLA_SUB * u + (0 if rev else GLA_SUB - 1) for j in range(nb)]
               for u in range(ns)]

    src_blocks = range(1, nb) if rev else range(nb - 1)
    e_blk = _bcast_rows(b, blk_end, GLA_BLK)
    kt = (k * jnp.exp2(e_blk - b)).astype(BF16)
    lhs1, rhs1 = [], []
    for j in src_blocks:
        lo, hi = GLA_BLK * j, GLA_BLK * (j + 1)
        later = slice(0, lo) if rev else slice(hi, c)
        e = b[blk_end[j]:blk_end[j] + 1, :]
        part = (qf[later] * jnp.exp2(b[later] - e)).astype(BF16)
        pad = jnp.zeros((c - part.shape[0], width), BF16)
        lhs1.append(jnp.concatenate([part, pad] if rev else [pad, part], axis=0))
        rhs1.append(_stack_rows([jnp.zeros((lo, width), BF16), kt[lo:hi],
                                 jnp.zeros((c - hi, width), BF16)]))

    e_sub = [_bcast_rows(b, sub_end[u], GLA_BLK) for u in range(ns)]
    e_own = e_sub[ns - 1]
    for u in range(ns - 2, -1, -1):
        e_own = jnp.where(sub == u, e_sub[u], e_own)
    kt2 = k * jnp.exp2(e_own - b)
    src_subs = range(1, ns) if rev else range(ns - 1)
    lhs2, rhs2 = [], []
    for u in src_subs:
        later = (sub < u) if rev else (sub > u)
        lhs2.append((qf * jnp.exp2(jnp.where(later, b - e_sub[u], NEG))).astype(BF16))
        rhs2.append(jnp.where(sub == u, kt2, 0.0).astype(BF16))

    e_end = b[0:1, :] if rev else b[c - 1:c, :]
    qd = (qf * jnp.exp2(b)).astype(BF16)
    kdec = (k * jnp.exp2(e_end - b)).astype(BF16)
    dec = jnp.exp2(e_end)
    yield

    a = []
    for p in pairs:
        h0, h1 = 2 * p, 2 * p + 1
        cat = lambda xs, h: jnp.concatenate([_head(x, h) for x in xs], axis=1)
        a1 = _dot_nt(jnp.concatenate([cat(lhs1, h0), cat(lhs1, h1)], axis=1),
                     _block_diag(cat(rhs1, h0), cat(rhs1, h1)))
        a2 = _dot_nt(jnp.concatenate([cat(lhs2, h0), cat(lhs2, h1)], axis=1),
                     _block_diag(cat(rhs2, h0), cat(rhs2, h1)))
        for e in range(2):
            cols = slice(e * c, (e + 1) * c)
            ah = a1[:, cols] + jnp.where(cst["same_blk"], a2[:, cols], 0.0)
            for dist in range(GLA_SUB):
                ah = jnp.where(cst["code"] == dist, p3[p][dist * c:(dist + 1) * c, cols], ah)
            a.append(ah.astype(BF16))
    yield

    outs = []
    for p in pairs:
        h0, h1 = 2 * p, 2 * p + 1
        st0, st1 = state[h0], state[h1]
        intra = _dot(jnp.concatenate([a[h0], a[h1]], axis=1),
                     _block_diag(_head(v, h0), _head(v, h1)))
        inter = _dot_nt(_pair(qd, p), _block_diag(st0.astype(BF16), st1.astype(BF16)))
        outs.append(intra + inter)
        state[h0] = st0 * _head(dec, h0) + _dot_tn(_head(v, h0), _head(kdec, h0))
        state[h1] = st1 * _head(dec, h1) + _dot_tn(_head(v, h1), _head(kdec, h1))
    emit(jnp.concatenate(outs, axis=1))


def _lower_bound(lb_ref):
    p = lb_ref[...]
    e = jnp.exp(p - jnp.max(p, axis=0, keepdims=True))
    return e[0:1, :] / jnp.sum(e, axis=0, keepdims=True)


def _run_staggered(gens):
    done = [False] * len(gens)
    rounds = 0
    while not all(done):
        for i, gen in enumerate(gens):
            if i <= rounds and not done[i]:
                try:
                    next(gen)
                except StopIteration:
                    done[i] = True
        rounds += 1


def _scan_steps(q_ref, v_ref, f_ref, lb, state, o_ref, rev):
    n = q_ref.shape[0] // GLA_CHUNK
    cst = _gla_consts(GLA_CHUNK, rev)
    gens = []
    for ci in (range(n - 1, -1, -1) if rev else range(n)):
        rows = slice(ci * GLA_CHUNK, (ci + 1) * GLA_CHUNK)

        def emit(o, rows=rows):
            o_ref[rows, :] = o

        gens.append(_gla_chunk_steps(q_ref.at[rows, :], v_ref.at[rows, :], f_ref.at[rows, :],
                                     lb, state, emit, cst, rev))
    steps = []

    def advance(gen):
        def run():
            next(gen, None)
        return run

    depth = 4
    for rnd in range(len(gens) + depth - 1):
        for i, gen in enumerate(gens):
            if 0 <= rnd - i < depth:
                steps.append(advance(gen))
    return steps


def _fwd_kernel(x_ref, w13_ref, w2_ref, lng_ref, lnb_ref, win_ref, lbf_ref,
                x1_ref, qa_ref, kv_ref, qh_ref, ih_ref, gh_ref, fb_ref, of_ref,
                qbuf_ref, ibuf_ref, fbuf_ref, st_ref, xb_ref, acc_ref,
                *, tiles_per_seq):
    s = pl.program_id(0)
    cur = lax.rem(s, 2)
    prv = 1 - cur

    @pl.when(s == 0)
    def _():
        qbuf_ref[...] = jnp.zeros(qbuf_ref.shape, BF16)
        ibuf_ref[...] = jnp.zeros(ibuf_ref.shape, BF16)
        fbuf_ref[...] = jnp.zeros(fbuf_ref.shape, F32)
        st_ref[...] = jnp.zeros(st_ref.shape, F32)

    restart = lax.rem(s - 1, tiles_per_seq) == 0
    state = [jnp.where(restart, 0.0, st_ref[h]) for h in range(HG_HEADS)]
    scan = _scan_steps(qbuf_ref.at[prv], ibuf_ref.at[prv], fbuf_ref.at[prv],
                       _lower_bound(lbf_ref), state, of_ref, False)

    xb_ref[...] = x_ref[...].astype(BF16)
    pre = _ffn_steps(xb_ref, w13_ref, w2_ref, acc_ref)

    def norm():
        x1 = _layer_norm(ALPHA * x_ref[...] + 0.5 * acc_ref[...], lng_ref[...], lnb_ref[...])
        x1_ref[...] = x1
        xb_ref[...] = x1.astype(BF16)

    def proj(off, width, dtype, out_ref, keep_ref):
        def run():
            val = _dot(xb_ref[...], win_ref[:, off:off + width]).astype(dtype)
            if out_ref is not None:
                out_ref[...] = val
            if keep_ref is not None:
                keep_ref[cur] = val
        return run

    pre.append(norm)
    off = 0
    for out_ref, keep_ref, width, dtype in (
            (qa_ref, None, ATTN_WIDTH, BF16), (kv_ref, None, 2 * KV_WIDTH, BF16),
            (qh_ref, qbuf_ref, HG_WIDTH, BF16), (ih_ref, ibuf_ref, HG_WIDTH, BF16),
            (gh_ref, None, HG_WIDTH, BF16), (fb_ref, None, HG_WIDTH, F32),
            (None, fbuf_ref, HG_WIDTH, F32)):
        pre.append(proj(off, width, dtype, out_ref, keep_ref))
        off += width

    for step in _interleave(pre, scan):
        step()
    for h in range(HG_HEADS):
        st_ref[h] = state[h]


def _fwd_call(x, w13, w2, lng, lnb, win, lbf, tm, tiles_per_seq):
    m = x.shape[0]
    n = m // tm
    cur = lambda w: pl.BlockSpec((tm, w), lambda s: (jnp.minimum(s, n - 1), 0))
    prev = lambda w: pl.BlockSpec((tm, w), lambda s: (jnp.maximum(s - 1, 0), 0))
    widths = (D_MODEL, ATTN_WIDTH, 2 * KV_WIDTH, HG_WIDTH, HG_WIDTH, HG_WIDTH, HG_WIDTH)
    dtypes = (F32, BF16, BF16, BF16, BF16, BF16, F32)
    out_shape = [jax.ShapeDtypeStruct((m, w), d) for w, d in zip(widths, dtypes)]
    out_shape.append(jax.ShapeDtypeStruct((m, HG_WIDTH), F32))
    kern = functools.partial(_fwd_kernel, tiles_per_seq=tiles_per_seq)
    return pl.pallas_call(
        kern,
        grid=(n + 1,),
        in_specs=[cur(D_MODEL), _resident(w13.shape), _resident(w2.shape),
                  _resident(lng.shape), _resident(lnb.shape), _resident(win.shape),
                  _resident(lbf.shape)],
        out_specs=[cur(w) for w in widths] + [prev(HG_WIDTH)],
        out_shape=out_shape,
        scratch_shapes=[pltpu.VMEM((2, tm, HG_WIDTH), BF16), pltpu.VMEM((2, tm, HG_WIDTH), BF16),
                        pltpu.VMEM((2, tm, HG_WIDTH), F32),
                        pltpu.VMEM((HG_HEADS, LANES, HG_DK), F32),
                        pltpu.VMEM((tm, D_MODEL), BF16), pltpu.VMEM((tm, D_MODEL), F32)],
        compiler_params=pltpu.CompilerParams(
            dimension_semantics=("arbitrary",), vmem_limit_bytes=VMEM_LIMIT),
        name="fwd",
    )(x, w13, w2, lng, lnb, win, lbf)


def _bwd_kernel(qh_ref, ih_ref, fb_ref, lbb_ref, x1_ref, oa_ref, of_ref, gh_ref, hg_ref,
                wout_ref, ln2g_ref, ln2b_ref, w13_ref, w2_ref, ln3g_ref, ln3b_ref,
                out_ref, ob_ref, st_ref, xb_ref, acc_ref, x2_ref, *, tiles_per_seq):
    s = pl.program_id(0)
    cur = lax.rem(s, 2)
    prv = 1 - cur

    @pl.when(s == 0)
    def _():
        ob_ref[...] = jnp.zeros(ob_ref.shape, F32)
        st_ref[...] = jnp.zeros(st_ref.shape, F32)

    def mix():
        o = of_ref[...] + ob_ref[prv]
        parts = []
        for h in range(HG_HEADS):
            oh = o[:, h * LANES:(h + 1) * LANES]
            ms = jnp.mean(oh * oh, axis=1, keepdims=True)
            parts.append(oh * lax.rsqrt(ms + EPS))
        gate = gh_ref[...].astype(F32)
        ohg = jnp.concatenate(parts, axis=1) * hg_ref[...] * (gate * jax.nn.sigmoid(gate))
        y = (_dot(oa_ref[...], wout_ref[:ATTN_WIDTH, :])
             + _dot(ohg.astype(BF16), wout_ref[ATTN_WIDTH:, :]))
        x2 = _layer_norm(ALPHA * x1_ref[...] + y, ln2g_ref[...], ln2b_ref[...])
        x2_ref[...] = x2
        xb_ref[...] = x2.astype(BF16)

    def final():
        out_ref[...] = _layer_norm(ALPHA * x2_ref[...] + 0.5 * acc_ref[...],
                                   ln3g_ref[...], ln3b_ref[...])

    post = [mix] + _ffn_steps(xb_ref, w13_ref, w2_ref, acc_ref) + [final]

    restart = lax.rem(s, tiles_per_seq) == 0
    state = [jnp.where(restart, 0.0, st_ref[h]) for h in range(HG_HEADS)]
    scan = _scan_steps(qh_ref, ih_ref, fb_ref, _lower_bound(lbb_ref), state,
                       ob_ref.at[cur], True)
    for step in _interleave(post, scan):
        step()
    for h in range(HG_HEADS):
        st_ref[h] = state[h]


def _bwd_call(qh, ih, fb, lbb, x1, oa, of, gh, hg, wout, ln2g, ln2b, w13, w2, ln3g, ln3b,
              tm, tiles_per_seq):
    m = x1.shape[0]
    n = m // tm
    nt = tiles_per_seq

    def tile(s):
        return (s // nt) * nt + (nt - 1 - s % nt)

    cur = lambda w: pl.BlockSpec((tm, w), lambda s: (tile(jnp.minimum(s, n - 1)), 0))
    prev = lambda w: pl.BlockSpec((tm, w), lambda s: (tile(jnp.maximum(s - 1, 0)), 0))
    kern = functools.partial(_bwd_kernel, tiles_per_seq=nt)
    return pl.pallas_call(
        kern,
        grid=(n + 1,),
        in_specs=[cur(HG_WIDTH), cur(HG_WIDTH), cur(HG_WIDTH), _resident(lbb.shape),
                  prev(D_MODEL), prev(ATTN_WIDTH), prev(HG_WIDTH), prev(HG_WIDTH),
                  _resident(hg.shape), _resident(wout.shape), _resident(ln2g.shape),
                  _resident(ln2b.shape), _resident(w13.shape), _resident(w2.shape),
                  _resident(ln3g.shape), _resident(ln3b.shape)],
        out_specs=prev(D_MODEL),
        out_shape=jax.ShapeDtypeStruct((m, D_MODEL), F32),
        scratch_shapes=[pltpu.VMEM((2, tm, HG_WIDTH), F32),
                        pltpu.VMEM((HG_HEADS, LANES, HG_DK), F32),
                        pltpu.VMEM((tm, D_MODEL), BF16), pltpu.VMEM((tm, D_MODEL), F32),
                        pltpu.VMEM((tm, D_MODEL), F32)],
        compiler_params=pltpu.CompilerParams(
            dimension_semantics=("arbitrary",), vmem_limit_bytes=VMEM_LIMIT),
        name="bwd",
    )(qh, ih, fb, lbb, x1, oa, of, gh, hg, wout, ln2g, ln2b, w13, w2, ln3g, ln3b)


def _prep_weights(ln_g, ln_b, ffn_w13, ffn_w2, w_in, attn_sink, attn_norm_g,
                  hg_lb_fwd, hg_lb_bwd, hg_norm_g, w_out):
    l = 0
    sizes = (ATTN_WIDTH, KV_WIDTH, KV_WIDTH, HG_WIDTH, HG_WIDTH, HG_WIDTH, HG_WIDTH, HG_WIDTH)
    offs = [0]
    for s in sizes:
        offs.append(offs[-1] + s)
    cols = [w_in[l][:, offs[i]:offs[i + 1]] for i in range(len(sizes))]
    q_a, k_a, v_a, q_h, f_f, f_b, i_h, g_h = cols
    win = jnp.concatenate([q_a * HEAD_DIM ** -0.5, k_a, v_a, q_h, i_h, g_h, f_b, f_f], axis=1)
    row = lambda a: a.reshape(1, -1).astype(F32)
    return dict(
        w13=[ffn_w13[l, i].astype(BF16) for i in range(2)],
        w2=[ffn_w2[l, i].astype(BF16) for i in range(2)],
        lng=[row(ln_g[l, i]) for i in range(3)],
        lnb=[row(ln_b[l, i]) for i in range(3)],
        win=win.astype(BF16),
        sink=attn_sink[l].astype(F32),
        ng=row(attn_norm_g[l]),
        lbf=hg_lb_fwd.astype(F32),
        lbb=hg_lb_bwd.astype(F32),
        hg=row(hg_norm_g[l]),
        wout=w_out[l].astype(BF16),
    )


def _largest_tile(n, cap, unit):
    t = min(cap, n)
    while n % t or t % unit:
        t -= unit
    return t


def _layer(x, w):
    bsz, t, d = x.shape
    m = bsz * t
    tm = _largest_tile(t, ROW_TILE, GLA_CHUNK)
    tq = _largest_tile(t, ATTN_TILE, WINDOW)
    x1, qa, kv, qh, ih, gh, fb, of = _fwd_call(
        x.reshape(m, d), w["w13"][0], w["w2"][0], w["lng"][0], w["lnb"][0], w["win"],
        w["lbf"], tm, t // tm)
    seq = lambda a: a.reshape(bsz, t, a.shape[-1])
    oa = _attn_call(w["sink"], seq(qa), seq(kv), w["ng"], tq)
    y = _bwd_call(qh, ih, fb, w["lbb"], x1, oa.reshape(m, ATTN_WIDTH), of, gh, w["hg"],
                  w["wout"], w["lng"][1], w["lnb"][1], w["w13"][1], w["w2"][1],
                  w["lng"][2], w["lnb"][2], tm, t // tm)
    return y.reshape(bsz, t, d)


def kernel(x_prompt, x_sample, ln_g, ln_b, ffn_w13, ffn_w2, w_in, attn_sink, attn_norm_g,
           hg_lb_fwd, hg_lb_bwd, hg_norm_g, w_out):
    w = _prep_weights(ln_g, ln_b, ffn_w13, ffn_w2, w_in, attn_sink, attn_norm_g,
                      hg_lb_fwd, hg_lb_bwd, hg_norm_g, w_out)
    return (_layer(x_prompt, w), _layer(x_sample, w))
```
